```python
import jax, jax.numpy as jnp
from jax import lax
import numpy as np

D_MODEL = 2048
BATCH = 2
SEQ = 8192
DEPTH = 4
DEC_BATCH = 8
DEC_SEQ = 64
PAST_LEN = 4096

CHUNK = 64
N_META = 16
H_A = 8
DK_A = 128
DV_A = 128
H_B = 8
DH_B = 128
W_A = H_A * DK_A
W_B = H_B * DH_B
MIX_WIDTH = H_A * DV_A + W_B
IN_WIDTH = 4 * W_A + 3 * W_B
SPLIT_POINTS = (W_A, 2 * W_A, 3 * W_A, 4 * W_A, 4 * W_A + W_B, 4 * W_A + 2 * W_B)
D_FF = 5632
N_EXPERTS = 8
TOP_K = 2
Q_BLOCK = 128
EPS = 1e-6
LB_FLOOR = 1e-20
N_DENSE = (DEPTH + 1) // 2
N_MOE = DEPTH // 2

kernel_name = 'hymba_hgrn2_stickbreak_stream_step'


def rmsnorm(x, g):
    xf = x.astype(jnp.float32)
    y = xf * lax.rsqrt(jnp.mean(xf * xf, axis=-1, keepdims=True) + EPS)
    return (y * g.astype(jnp.float32)).astype(x.dtype)


def head_rmsnorm(o, g):
    of = o.astype(jnp.float32)
    return of * lax.rsqrt(jnp.mean(of * of, axis=-1, keepdims=True) + EPS) * g.astype(jnp.float32)


def project(h, w_in_l, lb):
    B, T, _ = h.shape
    p = h @ w_in_l
    qa, fa, ia, ga, qb, kb, vb = jnp.split(p, SPLIT_POINTS, axis=-1)
    fa = fa.astype(jnp.float32)
    lf = jnp.logaddexp(jnp.log(jnp.maximum(lb, LB_FLOOR)), jnp.log1p(-lb) + jax.nn.log_sigmoid(fa))
    ka = (1.0 - lb) * jax.nn.sigmoid(-fa)
    ha = lambda a: a.reshape(B, T, H_A, -1)
    hb = lambda a: a.reshape(B, T, H_B, DH_B)
    return ha(qa), ha(lf), ha(ka), ha(ia), ga, hb(qb), hb(kb), hb(vb)


def hgrn_block(q, lf, k, v, s0):
    qf, kf, vf = q.astype(jnp.float32), k.astype(jnp.float32), v.astype(jnp.float32)
    L = q.shape[1]
    b = jnp.cumsum(lf, axis=1)
    causal = jnp.tril(jnp.ones((L, L), dtype=bool))[None, :, :, None, None]
    diff = b[:, :, None] - b[:, None, :]
    decay = jnp.where(causal, jnp.exp(jnp.where(causal, diff, 0.0)), 0.0)
    a = jnp.einsum('bthk,bshk,btshk->bhts', qf, kf, decay)
    o = jnp.einsum('bhts,bshv->bthv', a, vf) + jnp.einsum('bthk,bhkv->bthv', qf * jnp.exp(b), s0)
    b_last = b[:, -1]
    w = kf * jnp.exp(b_last[:, None] - b)
    s_new = jnp.exp(b_last)[..., None] * s0 + jnp.einsum('bshk,bshv->bhkv', w, vf)
    return o, s_new


def hgrn_scan(q, lf, k, v, s0, L):
    B, T = q.shape[:2]
    n = T // L
    chunks = lambda a: a.reshape((B, n, L) + a.shape[2:]).swapaxes(0, 1)

    def step(s, xs):
        o, s = hgrn_block(xs[0], xs[1], xs[2], xs[3], s)
        return s, o

    s, o = lax.scan(step, s0, (chunks(q), chunks(lf), chunks(k), chunks(v)))
    return o.swapaxes(0, 1).reshape(B, T, H_A, DV_A), s


def sb_attend(q, k, v, q_pos, k_pos):
    z = jnp.einsum('bqhd,bshd->bhqs', q.astype(jnp.float32), k.astype(jnp.float32)) * (DH_B ** -0.5)
    mask = (k_pos[None, :] < q_pos[:, None])[None, None]
    log_beta = jax.nn.log_sigmoid(z)
    log_rest = jnp.where(mask, jax.nn.log_sigmoid(-z), 0.0)
    after = lax.cumsum(log_rest, axis=3, reverse=True) - log_rest
    w = jnp.where(mask, jnp.exp(jnp.where(mask, log_beta + after, 0.0)), 0.0)
    return jnp.einsum('bhqs,bshd->bqhd', w, v.astype(jnp.float32))


def sb_prompt(q, k, v):
    B, T = q.shape[:2]
    nb = -(-T // Q_BLOCK)
    qp = jnp.pad(q, ((0, 0), (0, nb * Q_BLOCK - T), (0, 0), (0, 0)))
    qblocks = qp.reshape(B, nb, Q_BLOCK, H_B, DH_B).swapaxes(0, 1)
    pos = jnp.arange(nb * Q_BLOCK).reshape(nb, Q_BLOCK)
    k_pos = jnp.arange(T)
    o = lax.map(lambda a: sb_attend(a[0], k, v, a[1], k_pos), (qblocks, pos))
    return o.swapaxes(0, 1).reshape(B, nb * Q_BLOCK, H_B, DH_B)[:, :T]


def merge(oa, ga, ob, hgrn_g, sb_g, w_out_l):
    B, T = ga.shape[:2]
    ya = head_rmsnorm(oa, hgrn_g).reshape(B, T, -1) * jax.nn.silu(ga.astype(jnp.float32))
    yb = head_rmsnorm(ob, sb_g).reshape(B, T, -1)
    y = jnp.concatenate([ya, yb], axis=-1).astype(w_out_l.dtype)
    return y @ w_out_l


def swiglu(h, w1, w3, w2):
    return (jax.nn.silu(h @ w1) * (h @ w3)) @ w2


def moe_ffn(h, router, w1, w3, w2):
    probs = jax.nn.softmax((h @ router).astype(jnp.float32), axis=-1)
    top_p, top_i = lax.top_k(probs, TOP_K)
    top_p = top_p / jnp.sum(top_p, axis=-1, keepdims=True)
    gates = jnp.sum(jax.nn.one_hot(top_i, N_EXPERTS, dtype=jnp.float32) * top_p[..., None], axis=-2)
    y = jnp.zeros(h.shape, jnp.float32)
    for e in range(N_EXPERTS):
        y = y + gates[..., e:e + 1] * swiglu(h, w1[e], w3[e], w2[e]).astype(jnp.float32)
    return y.astype(h.dtype)


def channel_mixer(h, l, dense_w1, dense_w3, dense_w2, moe_router, moe_w1, moe_w3, moe_w2):
    if l % 2 == 0:
        i = l // 2
        return swiglu(h, dense_w1[i], dense_w3[i], dense_w2[i]).astype(h.dtype)
    i = l // 2
    return moe_ffn(h, moe_router[i], moe_w1[i], moe_w3[i], moe_w2[i])


def setup_inputs(seed: int = 0) -> dict:
    key = jax.random.key(seed)
    ks = jax.random.split(key, 24)
    f32 = jnp.float32
    nrm = lambda k, shape, scale: jax.random.normal(k, shape, f32) * scale
    gain = lambda k, shape: 1.0 + 0.02 * jax.random.normal(k, shape, f32)
    n_cache = N_META + PAST_LEN
    return {
        'x_prompt': nrm(ks[0], (BATCH, SEQ, D_MODEL), 1.0),
        'x_sample': nrm(ks[1], (DEC_BATCH, DEC_SEQ, D_MODEL), 1.0),
        'cache_k': nrm(ks[2], (DEPTH, DEC_BATCH, n_cache, H_B, DH_B), 1.0),
        'cache_v': nrm(ks[3], (DEPTH, DEC_BATCH, n_cache, H_B, DH_B), 1.0),
        'state_hgrn': nrm(ks[4], (DEPTH, DEC_BATCH, H_A, DK_A, DV_A), 0.5),
        'meta_tokens': nrm(ks[5], (N_META, D_MODEL), 1.0),
        'mix_norm': gain(ks[6], (DEPTH, D_MODEL)),
        'w_in': nrm(ks[7], (DEPTH, D_MODEL, IN_WIDTH), D_MODEL ** -0.5),
        'lb_logits': nrm(ks[8], (DEPTH, W_A), 0.1),
        'hgrn_norm': gain(ks[9], (DEPTH, DV_A)),
        'sb_norm': gain(ks[10], (DEPTH, DH_B)),
        'w_out': nrm(ks[11], (DEPTH, MIX_WIDTH, D_MODEL), MIX_WIDTH ** -0.5),
        'ffn_norm': gain(ks[12], (DEPTH, D_MODEL)),
        'dense_w1': nrm(ks[13], (N_DENSE, D_MODEL, D_FF), D_MODEL ** -0.5),
        'dense_w3': nrm(ks[14], (N_DENSE, D_MODEL, D_FF), D_MODEL ** -0.5),
        'dense_w2': nrm(ks[15], (N_DENSE, D_FF, D_MODEL), D_FF ** -0.5),
        'moe_router': nrm(ks[16], (N_MOE, D_MODEL, N_EXPERTS), D_MODEL ** -0.5),
        'moe_w1': nrm(ks[17], (N_MOE, N_EXPERTS, D_MODEL, D_FF), D_MODEL ** -0.5),
        'moe_w3': nrm(ks[18], (N_MOE, N_EXPERTS, D_MODEL, D_FF), D_MODEL ** -0.5),
        'moe_w2': nrm(ks[19], (N_MOE, N_EXPERTS, D_FF, D_MODEL), D_FF ** -0.5),
        'final_norm': gain(ks[20], (D_MODEL,)),
    }


def reference(x_prompt, x_sample, cache_k, cache_v, state_hgrn, meta_tokens, mix_norm, w_in, lb_logits,
              hgrn_norm, sb_norm, w_out, ffn_norm, dense_w1, dense_w3, dense_w2, moe_router, moe_w1,
              moe_w3, moe_w2, final_norm):
    f32 = jnp.float32
    B = x_prompt.shape[0]
    meta = jnp.broadcast_to(meta_tokens.astype(x_prompt.dtype)[None], (B, N_META, D_MODEL))
    xp = jnp.concatenate([meta, x_prompt], axis=1)
    xs = x_sample
    p_lb = jax.nn.softmax(lb_logits.astype(f32), axis=0)
    lb_all = jnp.cumsum(p_lb, axis=0) - p_lb[0]
    n_cache = cache_k.shape[2]
    dec_len = xs.shape[1]
    q_pos_s = n_cache + jnp.arange(dec_len)
    k_pos_s = jnp.arange(n_cache + dec_len)
    kp, vp, sp, ksl, vsl, ssl = [], [], [], [], [], []
    for l in range(DEPTH):
        h = rmsnorm(xp, mix_norm[l])
        qa, lf, ka, va, ga, qb, kb, vb = project(h, w_in[l], lb_all[l])
        s0 = jnp.zeros((B, H_A, DK_A, DV_A), f32)
        oa_m, s = hgrn_block(qa[:, :N_META], lf[:, :N_META], ka[:, :N_META], va[:, :N_META], s0)
        oa_r, s = hgrn_scan(qa[:, N_META:], lf[:, N_META:], ka[:, N_META:], va[:, N_META:], s, CHUNK)
        oa = jnp.concatenate([oa_m, oa_r], axis=1)
        ob = sb_prompt(qb, kb, vb)
        xp = xp + merge(oa, ga, ob, hgrn_norm[l], sb_norm[l], w_out[l]).astype(xp.dtype)
        kp.append(kb)
        vp.append(vb)
        sp.append(s.astype(x_prompt.dtype))
        h = rmsnorm(xs, mix_norm[l])
        qa, lf, ka, va, ga, qb, kb, vb = project(h, w_in[l], lb_all[l])
        oa, s = hgrn_block(qa, lf, ka, va, state_hgrn[l].astype(f32))
        k_all = jnp.concatenate([cache_k[l].astype(kb.dtype), kb], axis=1)
        v_all = jnp.concatenate([cache_v[l].astype(vb.dtype), vb], axis=1)
        ob = sb_attend(qb, k_all, v_all, q_pos_s, k_pos_s)
        xs = xs + merge(oa, ga, ob, hgrn_norm[l], sb_norm[l], w_out[l]).astype(xs.dtype)
        ksl.append(kb)
        vsl.append(vb)
        ssl.append(s.astype(state_hgrn.dtype))
        xp = xp + channel_mixer(rmsnorm(xp, ffn_norm[l]), l, dense_w1, dense_w3, dense_w2,
                                moe_router, moe_w1, moe_w3, moe_w2)
        xs = xs + channel_mixer(rmsnorm(xs, ffn_norm[l]), l, dense_w1, dense_w3, dense_w2,
                                moe_router, moe_w1, moe_w3, moe_w2)
    y_prompt = rmsnorm(xp, final_norm)[:, N_META:]
    y_sample = rmsnorm(xs, final_norm)
    return (y_prompt, y_sample, jnp.stack(kp), jnp.stack(vp), jnp.stack(sp),
            jnp.stack(ksl), jnp.stack(vsl), jnp.stack(ssl))
```

```python
import functools

import numpy as np
import jax
import jax.numpy as jnp
from jax import lax
from jax.experimental import pallas as pl
from jax.experimental.pallas import tpu as pltpu

F32 = jnp.float32
BF16 = jnp.bfloat16
EPS = 1e-6
LB_FLOOR = 1e-20
HEAD = 128
CHUNK = 128
LEVELS = 7
N_SEG = 7
TOKEN_TILE = 512
COL_TILE = 512
FF_TILE = 512
VMEM_LIMIT = 56 * 1024 * 1024
SB_EXIT = -104.0


def _round_up(x, m):
    return (x + m - 1) // m * m


def _params(sem):
    return pltpu.CompilerParams(dimension_semantics=sem, vmem_limit_bytes=VMEM_LIMIT)


def _dot(a, b):
    return jnp.dot(a, b, preferred_element_type=F32)


def _dot_nt(a, b):
    return lax.dot_general(a, b, (((1,), (1,)), ((), ())), preferred_element_type=F32)


def _split2(x):
    hi = x.astype(BF16)
    lo = (x - hi.astype(F32)).astype(BF16)
    return hi, lo


def _rms_rows(x, g):
    return x * lax.rsqrt(jnp.mean(x * x, axis=-1, keepdims=True) + EPS) * g


def _inproj_kernel(x_ref, g_ref, w_ref, o_ref, h_ref):
    @pl.when(pl.program_id(1) == 0)
    def _():
        h_ref[...] = _rms_rows(x_ref[...], g_ref[...]).astype(BF16)

    o_ref[...] = _dot(h_ref[...], w_ref[...])


def _inproj(x, gain, w_bf, layer, width):
    n, d = x.shape
    per_seg = width // COL_TILE
    grid = (pl.cdiv(n, TOKEN_TILE), N_SEG * per_seg)
    return pl.pallas_call(
        _inproj_kernel,
        grid=grid,
        in_specs=[
            pl.BlockSpec((TOKEN_TILE, d), lambda i, j: (i, 0)),
            pl.BlockSpec((None, 1, d), lambda i, j: (layer, 0, 0)),
            pl.BlockSpec((None, d, COL_TILE), lambda i, j: (layer, 0, j)),
        ],
        out_specs=pl.BlockSpec((None, TOKEN_TILE, COL_TILE), lambda i, j: (j // per_seg, i, j % per_seg)),
        out_shape=jax.ShapeDtypeStruct((N_SEG, n, width), F32),
        scratch_shapes=[pltpu.VMEM((TOKEN_TILE, d), BF16)],
        compiler_params=_params(("parallel", "arbitrary")),
        name="inproj",
    )(x, gain, w_bf)


def _hgrn_masks():
    c = CHUNK
    t = np.arange(c)[:, None]
    j = np.arange(c)[None, :]
    blocks = [(j <= t)]
    for l in range(LEVELS):
        hs = 1 << l
        mid = (t >> (l + 1) << (l + 1)) + hs
        upper = ((t >> l) & 1) == 1
        blocks.append(np.where(upper, (j >= mid) & (j <= t), (j > t) & (j <= mid - 1)))
    blocks.append(j > t)
    return np.concatenate(blocks, axis=0).astype(np.float32)


def _hgrn_chunk(q, fa, v, ga, valid, lbc, gain, m_ref, st_ref):
    log_lb, log1m_lb, one_m_lb = lbc
    e = jnp.exp(-jnp.abs(fa))
    log_sig = jnp.minimum(fa, 0.0) - jnp.log1p(e)
    bterm = log1m_lb + log_sig
    lf = jnp.maximum(log_lb, bterm) + jnp.log1p(jnp.exp(-jnp.abs(log_lb - bterm)))
    k = one_m_lb * (jnp.where(fa > 0.0, e, 1.0) / (1.0 + e))
    if valid is not None:
        lf = jnp.where(valid, lf, 0.0)
        k = jnp.where(valid, k, 0.0)

    lf_hi, lf_lo = _split2(lf)
    m = m_ref[...]
    dall = _dot(m, lf_hi) + _dot(m, lf_lo)
    b = dall[0:CHUNK]
    d_last = dall[(LEVELS + 1) * CHUNK:(LEVELS + 2) * CHUNK]

    row = lax.broadcasted_iota(jnp.int32, (CHUNK, HEAD), 0)
    ti = lax.broadcasted_iota(jnp.int32, (CHUNK, CHUNK), 0)
    si = lax.broadcasted_iota(jnp.int32, (CHUNK, CHUNK), 1)
    top_bit = 31 - lax.clz(ti ^ si)
    owner = jnp.where(si < ti, top_bit, jnp.where(si == ti, -1, -2))

    q_bf = q.astype(BF16)
    a = jnp.where(owner == -1, _dot_nt(q_bf, k.astype(BF16)), 0.0)
    for l in range(LEVELS):
        upper = ((row >> l) & 1) == 1
        z = (jnp.where(upper, q, k) * jnp.exp(dall[(l + 1) * CHUNK:(l + 2) * CHUNK])).astype(BF16)
        a = jnp.where(owner == l, _dot_nt(z, z), a)

    st = st_ref[...]
    v_bf = v.astype(BF16)
    o = _dot(a.astype(BF16), v_bf) + _dot_nt((q * jnp.exp(b)).astype(BF16), st.astype(BF16))
    w = (k * jnp.exp(d_last)).astype(BF16)
    st_ref[...] = st * jnp.exp(b[CHUNK - 1:CHUNK, :]) + _dot(v.T.astype(BF16), w)

    y = _rms_rows(o, gain) * (ga / (1.0 + jnp.exp(-ga)))
    return y.astype(BF16)


def _hgrn_kernel(*refs, rows, t_valid, has_s0, has_alias):
    q_ref, f_ref, i_ref, g_ref, lb_ref, gain_ref, m_ref = refs[:7]
    pos = 7
    s0_ref = None
    if has_s0:
        s0_ref = refs[pos]
        pos += 1
    if has_alias:
        pos += 1
    y_ref, s_ref, st_ref = refs[pos:pos + 3]
    r = pl.program_id(2)

    @pl.when(r == 0)
    def _():
        if has_s0:
            st_ref[...] = s0_ref[...].T
        else:
            st_ref[...] = jnp.zeros_like(st_ref)

    lb = lb_ref[...]
    lbc = (jnp.log(jnp.maximum(lb, LB_FLOOR)), jnp.log1p(-lb), 1.0 - lb)
    gain = gain_ref[...]
    row = lax.broadcasted_iota(jnp.int32, (CHUNK, HEAD), 0)

    if rows < CHUNK:
        pad = jnp.zeros((CHUNK - rows, HEAD), F32)
        ld = lambda ref: jnp.concatenate([ref[...], pad], axis=0)
        y = _hgrn_chunk(ld(q_ref), ld(f_ref), ld(i_ref), ld(g_ref), row < t_valid, lbc, gain, m_ref, st_ref)
        y_ref[...] = y[:rows]
    else:
        def body(c, carry):
            sl = pl.ds(pl.multiple_of(c * CHUNK, CHUNK), CHUNK)
            valid = (r * rows + c * CHUNK + row) < t_valid
            y_ref[sl, :] = _hgrn_chunk(q_ref[sl, :], f_ref[sl, :], i_ref[sl, :], g_ref[sl, :],
                                       valid, lbc, gain, m_ref, st_ref)
            return carry

        lax.fori_loop(0, rows // CHUNK, body, 0)

    @pl.when(r == pl.num_programs(2) - 1)
    def _():
        s_ref[...] = st_ref[...].T


def _hgrn(p7, lb, gain, masks, layer, *, n_seq, rows, n_blk, row_blk0, t_valid, s0=None, y_prev=None):
    _, n, width = p7.shape
    heads = width // HEAD
    seg_spec = lambda seg: pl.BlockSpec((None, rows, HEAD),
                                        lambda b, h, r: (seg, row_blk0 + b * n_blk + r, h))
    in_specs = [seg_spec(0), seg_spec(1), seg_spec(2), seg_spec(3),
                pl.BlockSpec((None, 1, HEAD), lambda b, h, r: (layer, 0, h)),
                pl.BlockSpec((None, 1, HEAD), lambda b, h, r: (layer, 0, 0)),
                pl.BlockSpec(masks.shape, lambda b, h, r: (0, 0))]
    args = [p7, p7, p7, p7, lb, gain, masks]
    aliases = {}
    if s0 is not None:
        in_specs.append(pl.BlockSpec((None, None, None, HEAD, HEAD), lambda b, h, r: (layer, b, h, 0, 0)))
        args.append(s0)
    if y_prev is not None:
        in_specs.append(pl.BlockSpec(memory_space=pl.ANY))
        aliases = {len(args): 0}
        args.append(y_prev)
    kern = functools.partial(_hgrn_kernel, rows=rows, t_valid=t_valid,
                             has_s0=s0 is not None, has_alias=y_prev is not None)
    return pl.pallas_call(
        kern,
        grid=(n_seq, heads, n_blk),
        in_specs=in_specs,
        out_specs=[pl.BlockSpec((rows, HEAD), lambda b, h, r: (row_blk0 + b * n_blk + r, h)),
                   pl.BlockSpec((None, None, HEAD, HEAD), lambda b, h, r: (b, h, 0, 0))],
        out_shape=[jax.ShapeDtypeStruct((n, width), BF16),
                   jax.ShapeDtypeStruct((n_seq, heads, HEAD, HEAD), F32)],
        scratch_shapes=[pltpu.VMEM((HEAD, HEAD), F32)],
        input_output_aliases=aliases,
        compiler_params=_params(("parallel", "parallel", "arbitrary")),
        name="hgrn_sample" if y_prev is not None else "hgrn_prompt",
    )(*args)


def _sb_masks():
    j = np.arange(CHUNK)[:, None]
    s = np.arange(CHUNK)[None, :]
    return np.concatenate([np.ones((CHUNK, CHUNK)), (j > s)], axis=1).astype(np.float32)


def _sb_tile(q_bf, k_t, v_t, mask, u_ref, r_ref, acc_ref, scale):
    z = _dot_nt(q_bf, k_t.astype(BF16)) * scale
    log_beta = jnp.minimum(z, 0.0) - jnp.log1p(jnp.exp(-jnp.abs(z)))
    log_rest = log_beta - z
    if mask is not None:
        log_rest = jnp.where(mask, log_rest, 0.0)
    hi, lo = _split2(log_rest)
    u = u_ref[...]
    c = _dot(hi, u) + _dot(lo, u)
    r = r_ref[...]
    w = jnp.exp(log_beta + r + c[:, CHUNK:])
    if mask is not None:
        w = jnp.where(mask, w, 0.0)
    acc_ref[...] += _dot(w.astype(BF16), v_t.astype(BF16))
    r_ref[...] = r + c[:, :CHUNK]


def _sb_sweep(q_bf, k_ref, v_ref, first, u_ref, r_ref, acc_ref, scale, row0=0):
    def cond(c):
        j, worst = c
        return jnp.logical_and(j >= 0, worst > SB_EXIT)

    def body(c):
        j, _ = c
        sl = pl.ds(pl.multiple_of(row0 + j * CHUNK, 8), CHUNK)
        _sb_tile(q_bf, k_ref[sl, :], v_ref[sl, :], None, u_ref, r_ref, acc_ref, scale)
        return j - 1, jnp.max(r_ref[...])

    lax.while_loop(cond, body, (first, jnp.max(r_ref[...])))


def _sb_prompt_kernel(q_ref, k_ref, v_ref, gain_ref, u_ref, y_ref, r_ref, acc_ref, *, scale):
    qi = pl.program_id(2)
    q_bf = q_ref[...].astype(BF16)
    r_ref[...] = jnp.zeros_like(r_ref)
    acc_ref[...] = jnp.zeros_like(acc_ref)
    ti = lax.broadcasted_iota(jnp.int32, (CHUNK, CHUNK), 0)
    si = lax.broadcasted_iota(jnp.int32, (CHUNK, CHUNK), 1)
    sl = pl.ds(pl.multiple_of(qi * CHUNK, CHUNK), CHUNK)
    _sb_tile(q_bf, k_ref[sl, :], v_ref[sl, :], si < ti, u_ref, r_ref, acc_ref, scale)
    _sb_sweep(q_bf, k_ref, v_ref, qi - 1, u_ref, r_ref, acc_ref, scale)
    y_ref[...] = _rms_rows(acc_ref[...], gain_ref[...]).astype(BF16)


def _sb_prompt(p7, gain, u, layer, *, n_seq, t_pad):
    _, n, width = p7.shape
    heads = width // HEAD
    n_q = t_pad // CHUNK
    kern = functools.partial(_sb_prompt_kernel, scale=HEAD ** -0.5)
    return pl.pallas_call(
        kern,
        grid=(n_seq, heads, n_q),
        in_specs=[pl.BlockSpec((None, CHUNK, HEAD), lambda b, h, i: (4, b * n_q + i, h)),
                  pl.BlockSpec((None, t_pad, HEAD), lambda b, h, i: (5, b, h)),
                  pl.BlockSpec((None, t_pad, HEAD), lambda b, h, i: (6, b, h)),
                  pl.BlockSpec((None, 1, HEAD), lambda b, h, i: (layer, 0, 0)),
                  pl.BlockSpec(u.shape, lambda b, h, i: (0, 0))],
        out_specs=pl.BlockSpec((CHUNK, HEAD), lambda b, h, i: (b * n_q + i, h)),
        out_shape=jax.ShapeDtypeStruct((n, width), BF16),
        scratch_shapes=[pltpu.VMEM((CHUNK, CHUNK), F32), pltpu.VMEM((CHUNK, HEAD), F32)],
        compiler_params=_params(("parallel", "parallel", "arbitrary")),
        name="sb_prompt",
    )(p7, p7, p7, gain, u)


def _sb_sample_kernel(q_ref, kn_ref, vn_ref, kc_ref, vc_ref, gain_ref, u_ref, y_prev_ref, y_ref,
                      r_ref, acc_ref, *, scale, rows, n_cache):
    del y_prev_ref
    q_bf = q_ref[...].astype(BF16)
    r_ref[...] = jnp.zeros_like(r_ref)
    acc_ref[...] = jnp.zeros_like(acc_ref)
    ti = lax.broadcasted_iota(jnp.int32, (rows, CHUNK), 0)
    si = lax.broadcasted_iota(jnp.int32, (rows, CHUNK), 1)
    pad = jnp.zeros((CHUNK - rows, HEAD), F32)
    _sb_tile(q_bf, jnp.concatenate([kn_ref[...], pad], axis=0), jnp.concatenate([vn_ref[...], pad], axis=0),
             si < ti, u_ref, r_ref, acc_ref, scale)
    n_full, rem = divmod(n_cache, CHUNK)
    _sb_sweep(q_bf, kc_ref, vc_ref, n_full - 1, u_ref, r_ref, acc_ref, scale, row0=rem)
    if rem:
        @pl.when(jnp.max(r_ref[...]) > SB_EXIT)
        def _():
            _sb_tile(q_bf, kc_ref[0:CHUNK, :], vc_ref[0:CHUNK, :], si < rem, u_ref, r_ref, acc_ref, scale)
    y_ref[...] = _rms_rows(acc_ref[...], gain_ref[...]).astype(BF16)


def _sb_sample(p7, cache_k, cache_v, gain, u, y_prev, layer, *, n_seq, rows, row_blk0):
    _, n, width = p7.shape
    heads = width // HEAD
    n_cache = cache_k.shape[2]
    assert n_cache >= CHUNK and rows <= CHUNK
    kern = functools.partial(_sb_sample_kernel, scale=HEAD ** -0.5, rows=rows, n_cache=n_cache)
    new_spec = lambda seg: pl.BlockSpec((None, rows, HEAD), lambda b, h: (seg, row_blk0 + b, h))
    cache_spec = pl.BlockSpec((None, None, n_cache, HEAD), lambda b, h: (layer, b, 0, h))
    return pl.pallas_call(
        kern,
        grid=(n_seq, heads),
        in_specs=[new_spec(4), new_spec(5), new_spec(6), cache_spec, cache_spec,
                  pl.BlockSpec((None, 1, HEAD), lambda b, h: (layer, 0, 0)),
                  pl.BlockSpec(u.shape, lambda b, h: (0, 0)),
                  pl.BlockSpec(memory_space=pl.ANY)],
        out_specs=pl.BlockSpec((rows, HEAD), lambda b, h: (row_blk0 + b, h)),
        out_shape=jax.ShapeDtypeStruct((n, width), BF16),
        scratch_shapes=[pltpu.VMEM((rows, CHUNK), F32), pltpu.VMEM((rows, HEAD), F32)],
        input_output_aliases={7: 0},
        compiler_params=_params(("parallel", "parallel")),
        name="sb_sample",
    )(p7, p7, p7, cache_k, cache_v, gain, u, y_prev)


def _outproj_kernel(ya_ref, yb_ref, wa_ref, wb_ref, x_ref, o_ref):
    o_ref[...] = x_ref[...] + _dot(ya_ref[...], wa_ref[...]) + _dot(yb_ref[...], wb_ref[...])


def _outproj(ya, yb, w_bf, x, layer):
    n, d = x.shape
    width = ya.shape[1]
    return pl.pallas_call(
        _outproj_kernel,
        grid=(pl.cdiv(n, TOKEN_TILE), d // COL_TILE),
        in_specs=[pl.BlockSpec((TOKEN_TILE, width), lambda i, j: (i, 0)),
                  pl.BlockSpec((TOKEN_TILE, width), lambda i, j: (i, 0)),
                  pl.BlockSpec((None, None, width, COL_TILE), lambda i, j: (layer, 0, 0, j)),
                  pl.BlockSpec((None, None, width, COL_TILE), lambda i, j: (layer, 1, 0, j)),
                  pl.BlockSpec((TOKEN_TILE, COL_TILE), lambda i, j: (i, j))],
        out_specs=pl.BlockSpec((TOKEN_TILE, COL_TILE), lambda i, j: (i, j)),
        out_shape=jax.ShapeDtypeStruct((n, d), F32),
        input_output_aliases={4: 0},
        compiler_params=_params(("parallel", "arbitrary")),
        name="outproj",
    )(ya, yb, w_bf, w_bf, x)


def _swiglu_step(h, w1_ref, w3_ref, w2_ref, row_scale=None):
    a = _dot(h, w1_ref[...])
    b = _dot(h, w3_ref[...])
    g = (a / (1.0 + jnp.exp(-a))) * b
    y = _dot(g.astype(BF16), w2_ref[...])
    return y if row_scale is None else row_scale * y


def _ffn_kernel(x_ref, g_ref, w1_ref, w3_ref, w2_ref, o_ref, h_ref):
    @pl.when(pl.program_id(1) == 0)
    def _():
        x = x_ref[...]
        h_ref[...] = _rms_rows(x, g_ref[...]).astype(BF16)
        o_ref[...] = x

    o_ref[...] += _swiglu_step(h_ref[...], w1_ref, w3_ref, w2_ref)


def _ffn(x, gain, w1, w3, w2, layer, idx):
    n, d = x.shape
    ff = w1.shape[-1]
    return pl.pallas_call(
        _ffn_kernel,
        grid=(pl.cdiv(n, TOKEN_TILE), ff // FF_TILE),
        in_specs=[pl.BlockSpec((TOKEN_TILE, d), lambda i, f: (i, 0)),
                  pl.BlockSpec((None, 1, d), lambda i, f: (layer, 0, 0)),
                  pl.BlockSpec((None, d, FF_TILE), lambda i, f: (idx, 0, f)),
                  pl.BlockSpec((None, d, FF_TILE), lambda i, f: (idx, 0, f)),
                  pl.BlockSpec((None, FF_TILE, d), lambda i, f: (idx, f, 0))],
        out_specs=pl.BlockSpec((TOKEN_TILE, d), lambda i, f: (i, 0)),
        out_shape=jax.ShapeDtypeStruct((n, d), F32),
        scratch_shapes=[pltpu.VMEM((TOKEN_TILE, d), BF16)],
        compiler_params=_params(("parallel", "arbitrary")),
        name="ffn_dense",
    )(x, gain, w1, w3, w2)


def _route(h, rt_ref, n_experts):
    h_hi, h_lo = _split2(h)
    r = rt_ref[...]
    r_hi, r_lo = _split2(r)
    logits = _dot(h_hi, r_hi) + _dot(h_lo, r_hi) + _dot(h_hi, r_lo)
    lane = lax.broadcasted_iota(jnp.int32, logits.shape, 1)
    live = lane < n_experts
    logits = jnp.where(live, logits, -jnp.inf)
    ex = jnp.exp(logits - jnp.max(logits, axis=-1, keepdims=True))
    p = ex / jnp.sum(ex, axis=-1, keepdims=True)
    p = jnp.where(live, p, -1.0)
    p1 = jnp.max(p, axis=-1, keepdims=True)
    i1 = jnp.min(jnp.where(p == p1, lane, HEAD), axis=-1, keepdims=True)
    rest = jnp.where(lane == i1, -1.0, p)
    p2 = jnp.max(rest, axis=-1, keepdims=True)
    i2 = jnp.min(jnp.where(rest == p2, lane, HEAD), axis=-1, keepdims=True)
    tot = p1 + p2
    return jnp.where(lane == i1, p1 / tot, 0.0) + jnp.where(lane == i2, p2 / tot, 0.0)


def _moe_kernel(x_ref, g_ref, rt_ref, w1_ref, w3_ref, w2_ref, o_ref, h_ref, gate_ref, *, n_experts):
    e = pl.program_id(1)
    f = pl.program_id(2)

    @pl.when(jnp.logical_and(e == 0, f == 0))
    def _():
        x = x_ref[...]
        h = _rms_rows(x, g_ref[...])
        h_ref[...] = h.astype(BF16)
        gate_ref[...] = _route(h, rt_ref, n_experts)
        o_ref[...] = x

    lane = lax.broadcasted_iota(jnp.int32, gate_ref.shape, 1)
    gate = jnp.sum(jnp.where(lane == e, gate_ref[...], 0.0), axis=-1, keepdims=True)
    o_ref[...] += _swiglu_step(h_ref[...], w1_ref, w3_ref, w2_ref, row_scale=gate)


def _moe(x, gain, router, w1, w3, w2, layer, idx):
    n, d = x.shape
    n_experts, _, ff = w1.shape[1:]
    kern = functools.partial(_moe_kernel, n_experts=n_experts)
    return pl.pallas_call(
        kern,
        grid=(pl.cdiv(n, TOKEN_TILE), n_experts, ff // FF_TILE),
        in_specs=[pl.BlockSpec((TOKEN_TILE, d), lambda i, e, f: (i, 0)),
                  pl.BlockSpec((None, 1, d), lambda i, e, f: (layer, 0, 0)),
                  pl.BlockSpec((None, d, HEAD), lambda i, e, f: (idx, 0, 0)),
                  pl.BlockSpec((None, None, d, FF_TILE), lambda i, e, f: (idx, e, 0, f)),
                  pl.BlockSpec((None, None, d, FF_TILE), lambda i, e, f: (idx, e, 0, f)),
                  pl.BlockSpec((None, None, FF_TILE, d), lambda i, e, f: (idx, e, f, 0))],
        out_specs=pl.BlockSpec((TOKEN_TILE, d), lambda i, e, f: (i, 0)),
        out_shape=jax.ShapeDtypeStruct((n, d), F32),
        scratch_shapes=[pltpu.VMEM((TOKEN_TILE, d), BF16), pltpu.VMEM((TOKEN_TILE, HEAD), F32)],
        compiler_params=_params(("parallel", "arbitrary", "arbitrary")),
        name="moe_dense",
    )(x, gain, router, w1, w3, w2)


def _final_norm_kernel(x_ref, g_ref, o_ref):
    o_ref[...] = _rms_rows(x_ref[...], g_ref[...])


def _final_norm(x, gain):
    n, d = x.shape
    return pl.pallas_call(
        _final_norm_kernel,
        grid=(pl.cdiv(n, TOKEN_TILE),),
        in_specs=[pl.BlockSpec((TOKEN_TILE, d), lambda i: (i, 0)),
                  pl.BlockSpec((1, d), lambda i: (0, 0))],
        out_specs=pl.BlockSpec((TOKEN_TILE, d), lambda i: (i, 0)),
        out_shape=jax.ShapeDtypeStruct((n, d), F32),
        compiler_params=_params(("parallel",)),
        name="final_norm",
    )(x, gain)


def _hgrn_rows(t_pad):
    n = t_pad // CHUNK
    best = max(c for c in range(1, n + 1) if n % c == 0 and c * CHUNK <= 2048)
    return best * CHUNK


def kernel(x_prompt, x_sample, cache_k, cache_v, state_hgrn, meta_tokens, mix_norm, w_in, lb_logits, hgrn_norm, sb_norm, w_out, ffn_norm, dense_w1, dense_w3, dense_w2, moe_router, moe_w1, moe_w3, moe_w2, final_norm):
    n_seq, seq, d = x_prompt.shape
    dec_b, dec_s, _ = x_sample.shape
    depth = w_in.shape[0]
    n_meta = meta_tokens.shape[0]
    heads = cache_k.shape[3]
    width = heads * HEAD
    n_cache = cache_k.shape[2]
    t_real = n_meta + seq
    n_sample = dec_b * dec_s
    t_pad = next(t for t in range(_round_up(t_real, CHUNK), t_real + TOKEN_TILE * CHUNK, CHUNK)
                 if (n_seq * t + n_sample) % TOKEN_TILE == 0)
    n_prompt = n_seq * t_pad
    assert w_in.shape[2] == N_SEG * width and n_prompt % dec_s == 0 and dec_s % 8 == 0

    meta = jnp.broadcast_to(meta_tokens.astype(F32)[None], (n_seq, n_meta, d))
    slab = jnp.concatenate([meta, x_prompt, jnp.zeros((n_seq, t_pad - t_real, d), F32)], axis=1)
    x = jnp.concatenate([slab.reshape(n_prompt, d), x_sample.reshape(n_sample, d)], axis=0)

    p_lb = jax.nn.softmax(lb_logits.astype(F32), axis=0)
    lb_all = (jnp.cumsum(p_lb, axis=0) - p_lb[0]).reshape(depth, 1, width)

    w_in_bf = w_in.astype(BF16)
    w_out_bf = w_out.astype(BF16).reshape(depth, 2, width, d)
    dense_bf = [w.astype(BF16) for w in (dense_w1, dense_w3, dense_w2)]
    moe_bf = [w.astype(BF16) for w in (moe_w1, moe_w3, moe_w2)]
    n_experts = moe_router.shape[-1]
    router = jnp.pad(moe_router.astype(F32), ((0, 0), (0, 0), (0, HEAD - n_experts)))
    mix_g = mix_norm.reshape(depth, 1, d)
    ffn_g = ffn_norm.reshape(depth, 1, d)
    hgrn_g = hgrn_norm.reshape(depth, 1, HEAD)
    sb_g = sb_norm.reshape(depth, 1, HEAD)
    hgrn_m = jnp.asarray(_hgrn_masks(), BF16)
    sb_u = jnp.asarray(_sb_masks(), BF16)
    cache_k2 = cache_k.reshape(depth, dec_b, n_cache, width)
    cache_v2 = cache_v.reshape(depth, dec_b, n_cache, width)
    rows_p = _hgrn_rows(t_pad)

    kp, vp, sp, ks, vs, ss = [], [], [], [], [], []
    for l in range(depth):
        p7 = _inproj(x, mix_g, w_in_bf, l, width)
        ya, s_p = _hgrn(p7, lb_all, hgrn_g, hgrn_m, l, n_seq=n_seq, rows=rows_p, n_blk=t_pad // rows_p,
                        row_blk0=0, t_valid=t_real)
        ya, s_s = _hgrn(p7, lb_all, hgrn_g, hgrn_m, l, n_seq=dec_b, rows=dec_s, n_blk=1,
                        row_blk0=n_prompt // dec_s, t_valid=dec_s, s0=state_hgrn, y_prev=ya)
        yb = _sb_prompt(p7, sb_g, sb_u, l, n_seq=n_seq, t_pad=t_pad)
        yb = _sb_sample(p7, cache_k2, cache_v2, sb_g, sb_u, yb, l, n_seq=dec_b, rows=dec_s,
                        row_blk0=n_prompt // dec_s)
        x = _outproj(ya, yb, w_out_bf, x, l)
        if l % 2 == 0:
            x = _ffn(x, ffn_g, *dense_bf, l, l // 2)
        else:
            x = _moe(x, ffn_g, router, *moe_bf, l, l // 2)

        kv = lambda seg: p7[seg, :n_prompt].reshape(n_seq, t_pad, heads, HEAD)[:, :t_real]
        kv_s = lambda seg: p7[seg, n_prompt:n_prompt + n_sample].reshape(dec_b, dec_s, heads, HEAD)
        kp.append(kv(5))
        vp.append(kv(6))
        sp.append(s_p)
        ks.append(kv_s(5))
        vs.append(kv_s(6))
        ss.append(s_s)

    y = _final_norm(x, final_norm.reshape(1, d))
    y_prompt = y[:n_prompt].reshape(n_seq, t_pad, d)[:, n_meta:t_real]
    y_sample = y[n_prompt:n_prompt + n_sample].reshape(dec_b, dec_s, d)
    return (y_prompt, y_sample, jnp.stack(kp), jnp.stack(vp), jnp.stack(sp),
            jnp.stack(ks), jnp.stack(vs), jnp.stack(ss))
```

```python
import functools

import numpy as np
import jax
import jax.numpy as jnp
from jax import lax
from jax.experimental import pallas as pl
from jax.experimental.pallas import tpu as pltpu

F32 = jnp.float32
BF16 = jnp.bfloat16
EPS = 1e-6
LB_FLOOR = 1e-20
HEAD = 128
CHUNK = 128
LEVELS = 7
N_SEG = 7
TOKEN_TILE = 512
COL_TILE = 512
IN_COL_TILE = 1024
MOE_TILE = 512
FF_TILE = 512
VMEM_LIMIT = 56 * 1024 * 1024
HGRN_GROUP = 4
SB_GROUP = 6
SB_EXIT = -104.0
ROUTE_I1, ROUTE_I2, ROUTE_G1, ROUTE_G2 = 8, 9, 10, 11


def _round_up(x, m):
    return (x + m - 1) // m * m


def _params(sem):
    return pltpu.CompilerParams(dimension_semantics=sem, vmem_limit_bytes=VMEM_LIMIT)


def _dot(a, b):
    return jnp.dot(a, b, preferred_element_type=F32)


def _dot_nt(a, b):
    return lax.dot_general(a, b, (((1,), (1,)), ((), ())), preferred_element_type=F32)


def _split2(x):
    hi = x.astype(BF16)
    lo = (x - hi.astype(F32)).astype(BF16)
    return hi, lo


def _rms_rows(x, g):
    return x * lax.rsqrt(jnp.mean(x * x, axis=-1, keepdims=True) + EPS) * g


def _inproj_kernel(x_ref, g_ref, w_ref, o_ref, h_ref):
    @pl.when(pl.program_id(1) == 0)
    def _():
        h_ref[...] = _rms_rows(x_ref[...], g_ref[...]).astype(BF16)

    o_ref[...] = _dot(h_ref[...], w_ref[...])


def _inproj(x, gain, w_bf, layer, width):
    n, d = x.shape
    per_seg = width // IN_COL_TILE
    grid = (n // TOKEN_TILE, N_SEG * per_seg)
    return pl.pallas_call(
        _inproj_kernel,
        grid=grid,
        in_specs=[
            pl.BlockSpec((TOKEN_TILE, d), lambda i, j: (i, 0)),
            pl.BlockSpec((None, 1, d), lambda i, j: (layer, 0, 0)),
            pl.BlockSpec((None, d, IN_COL_TILE), lambda i, j: (layer, 0, j)),
        ],
        out_specs=pl.BlockSpec((None, TOKEN_TILE, IN_COL_TILE), lambda i, j: (j // per_seg, i, j % per_seg)),
        out_shape=jax.ShapeDtypeStruct((N_SEG, n, width), F32),
        scratch_shapes=[pltpu.VMEM((TOKEN_TILE, d), BF16)],
        compiler_params=_params(("parallel", "arbitrary")),
        name="inproj",
    )(x, gain, w_bf)


def _hgrn_masks():
    c = CHUNK
    t = np.arange(c)[:, None]
    j = np.arange(c)[None, :]
    blocks = [(j <= t)]
    for l in range(LEVELS):
        hs = 1 << l
        mid = (t >> (l + 1) << (l + 1)) + hs
        upper = ((t >> l) & 1) == 1
        blocks.append(np.where(upper, (j >= mid) & (j <= t), (j > t) & (j <= mid - 1)))
    blocks.append(j > t)
    return np.concatenate(blocks, axis=0).astype(np.float32)


def _hgrn_chunk(q, fa, v, ga, valid, lbc, gain, m_ref, st_ref):
    log_lb, log1m_lb, one_m_lb = lbc
    e = jnp.exp(-jnp.abs(fa))
    log_sig = jnp.minimum(fa, 0.0) - jnp.log1p(e)
    bterm = log1m_lb + log_sig
    lf = jnp.maximum(log_lb, bterm) + jnp.log1p(jnp.exp(-jnp.abs(log_lb - bterm)))
    k = one_m_lb * (jnp.where(fa > 0.0, e, 1.0) / (1.0 + e))
    if valid is not None:
        lf = jnp.where(valid, lf, 0.0)
        k = jnp.where(valid, k, 0.0)

    lf_hi, lf_lo = _split2(lf)
    m = m_ref[...]
    dall = _dot(m, lf_hi) + _dot(m, lf_lo)
    b = dall[0:CHUNK]
    d_last = dall[(LEVELS + 1) * CHUNK:(LEVELS + 2) * CHUNK]

    row = lax.broadcasted_iota(jnp.int32, (CHUNK, HEAD), 0)
    ti = lax.broadcasted_iota(jnp.int32, (CHUNK, CHUNK), 0)
    si = lax.broadcasted_iota(jnp.int32, (CHUNK, CHUNK), 1)
    top_bit = 31 - lax.clz(ti ^ si)
    owner = jnp.where(si < ti, top_bit, jnp.where(si == ti, -1, -2))

    q_bf = q.astype(BF16)
    a = jnp.where(owner == -1, _dot_nt(q_bf, k.astype(BF16)), 0.0)
    for l in range(LEVELS):
        upper = ((row >> l) & 1) == 1
        z = (jnp.where(upper, q, k) * jnp.exp(dall[(l + 1) * CHUNK:(l + 2) * CHUNK])).astype(BF16)
        a = jnp.where(owner == l, _dot_nt(z, z), a)

    st = st_ref[...]
    v_bf = v.astype(BF16)
    o = _dot(a.astype(BF16), v_bf) + _dot_nt((q * jnp.exp(b)).astype(BF16), st.astype(BF16))
    w = (k * jnp.exp(d_last)).astype(BF16)
    st_ref[...] = st * jnp.exp(b[CHUNK - 1:CHUNK, :]) + _dot(v.T.astype(BF16), w)

    y = _rms_rows(o, gain) * (ga / (1.0 + jnp.exp(-ga)))
    return y.astype(BF16)


def _hgrn_kernel(*refs, rows, t_valid, group, has_s0):
    q_ref, f_ref, i_ref, g_ref, lb_ref, gain_ref, m_ref = refs[:7]
    s0_ref = refs[7] if has_s0 else None
    y_ref, s_ref, st_ref = refs[-3:]
    r = pl.program_id(2)

    @pl.when(r == 0)
    def _():
        for hh in range(group):
            st_ref[hh] = s0_ref[hh].T if has_s0 else jnp.zeros((HEAD, HEAD), F32)

    gain = gain_ref[...]
    row = lax.broadcasted_iota(jnp.int32, (CHUNK, HEAD), 0)
    cols = [slice(hh * HEAD, (hh + 1) * HEAD) for hh in range(group)]
    lbcs = []
    for hh in range(group):
        lb = lb_ref[:, cols[hh]]
        lbcs.append((jnp.log(jnp.maximum(lb, LB_FLOOR)), jnp.log1p(-lb), 1.0 - lb))

    if rows < CHUNK:
        pad = jnp.zeros((CHUNK - rows, HEAD), F32)
        for hh in range(group):
            ld = lambda ref: jnp.concatenate([ref[:, cols[hh]], pad], axis=0)
            y = _hgrn_chunk(ld(q_ref), ld(f_ref), ld(i_ref), ld(g_ref), row < t_valid, lbcs[hh], gain,
                            m_ref, st_ref.at[hh])
            y_ref[:, cols[hh]] = y[:rows]
    else:
        def body(c, carry):
            sl = pl.ds(pl.multiple_of(c * CHUNK, CHUNK), CHUNK)
            valid = (r * rows + c * CHUNK + row) < t_valid
            for hh in range(group):
                y_ref[sl, cols[hh]] = _hgrn_chunk(q_ref[sl, cols[hh]], f_ref[sl, cols[hh]], i_ref[sl, cols[hh]],
                                                  g_ref[sl, cols[hh]], valid, lbcs[hh], gain, m_ref, st_ref.at[hh])
            return carry

        lax.fori_loop(0, rows // CHUNK, body, 0)

    @pl.when(r == pl.num_programs(2) - 1)
    def _():
        for hh in range(group):
            s_ref[hh] = st_ref[hh].T


def _hgrn(p7, lb, gain, masks, layer, *, n_seq, rows, n_blk, row_blk0, t_valid, s0=None):
    _, n, width = p7.shape
    heads = width // HEAD
    group = HGRN_GROUP
    gw = group * HEAD
    seg_spec = lambda seg: pl.BlockSpec((None, rows, gw),
                                        lambda b, h, r: (seg, row_blk0 + b * n_blk + r, h))
    in_specs = [seg_spec(0), seg_spec(1), seg_spec(2), seg_spec(3),
                pl.BlockSpec((None, 1, gw), lambda b, h, r: (layer, 0, h)),
                pl.BlockSpec((None, 1, HEAD), lambda b, h, r: (layer, 0, 0)),
                pl.BlockSpec(masks.shape, lambda b, h, r: (0, 0))]
    args = [p7, p7, p7, p7, lb, gain, masks]
    if s0 is not None:
        in_specs.append(pl.BlockSpec((None, None, group, HEAD, HEAD), lambda b, h, r: (layer, b, h, 0, 0)))
        args.append(s0)
    kern = functools.partial(_hgrn_kernel, rows=rows, t_valid=t_valid, group=group, has_s0=s0 is not None)
    return pl.pallas_call(
        kern,
        grid=(n_seq, heads // group, n_blk),
        in_specs=in_specs,
        out_specs=[pl.BlockSpec((rows, gw), lambda b, h, r: (b * n_blk + r, h)),
                   pl.BlockSpec((None, group, HEAD, HEAD), lambda b, h, r: (b, h, 0, 0))],
        out_shape=[jax.ShapeDtypeStruct((n_seq * n_blk * rows, width), BF16),
                   jax.ShapeDtypeStruct((n_seq, heads, HEAD, HEAD), F32)],
        scratch_shapes=[pltpu.VMEM((group, HEAD, HEAD), F32)],
        compiler_params=_params(("parallel", "parallel", "arbitrary")),
        name="hgrn_sample" if s0 is not None else "hgrn_prompt",
    )(*args)


def _sb_masks():
    j = np.arange(CHUNK)[:, None]
    s = np.arange(CHUNK)[None, :]
    return np.concatenate([np.ones((CHUNK, CHUNK)), (j > s)], axis=1).astype(np.float32)


def _sb_tile(q_bf, k_t, v_t, mask, u_ref, r_ref, acc_ref, scale):
    z = _dot_nt(q_bf, k_t.astype(BF16)) * scale
    log_beta = jnp.minimum(z, 0.0) - jnp.log1p(jnp.exp(-jnp.abs(z)))
    log_rest = log_beta - z
    if mask is not None:
        log_rest = jnp.where(mask, log_rest, 0.0)
    hi, lo = _split2(log_rest)
    u = u_ref[...]
    c = _dot(hi, u) + _dot(lo, u)
    r = r_ref[...]
    w = jnp.exp(log_beta + r + c[:, CHUNK:])
    if mask is not None:
        w = jnp.where(mask, w, 0.0)
    acc_ref[...] += _dot(w.astype(BF16), v_t.astype(BF16))
    r_ref[...] = r + c[:, :CHUNK]


def _sb_prompt_kernel(q_ref, k_ref, v_ref, gain_ref, u_ref, y_ref, r_ref, acc_ref, *, scale, sub):
    first = pl.program_id(2) * sub
    r_ref[...] = jnp.zeros_like(r_ref)
    acc_ref[...] = jnp.zeros_like(acc_ref)
    ti = lax.broadcasted_iota(jnp.int32, (CHUNK, CHUNK), 0)
    si = lax.broadcasted_iota(jnp.int32, (CHUNK, CHUNK), 1)
    q_bf = [q_ref[a * CHUNK:(a + 1) * CHUNK, :].astype(BF16) for a in range(sub)]

    def tiles(step, mask_of):
        for a in range(sub):
            j = first + a - step
            sl = pl.ds(pl.multiple_of(jnp.maximum(j, 0) * CHUNK, CHUNK), CHUNK)
            _sb_tile(q_bf[a], k_ref[sl, :], v_ref[sl, :], mask_of(j), u_ref, r_ref.at[a], acc_ref.at[a], scale)

    tiles(0, lambda j: si < ti)

    def cond(c):
        step, worst = c
        return jnp.logical_and(step < first + sub, worst > SB_EXIT)

    def body(c):
        step, _ = c
        tiles(step, lambda j: jnp.broadcast_to(j >= 0, (CHUNK, CHUNK)))
        return step + 1, jnp.max(r_ref[...])

    lax.while_loop(cond, body, (jnp.int32(1), jnp.max(r_ref[...])))
    gain = gain_ref[...]
    for a in range(sub):
        y_ref[a * CHUNK:(a + 1) * CHUNK, :] = _rms_rows(acc_ref[a], gain).astype(BF16)


def _sb_prompt(p7, gain, u, layer, *, n_seq, t_pad):
    _, n, width = p7.shape
    heads = width // HEAD
    n_q = t_pad // CHUNK
    sub = max(c for c in range(1, SB_GROUP + 1) if n_q % c == 0)
    n_g = n_q // sub
    kern = functools.partial(_sb_prompt_kernel, scale=HEAD ** -0.5, sub=sub)
    return pl.pallas_call(
        kern,
        grid=(n_seq, heads, n_g),
        in_specs=[pl.BlockSpec((None, sub * CHUNK, HEAD), lambda b, h, i: (4, b * n_g + i, h)),
                  pl.BlockSpec((None, t_pad, HEAD), lambda b, h, i: (5, b, h)),
                  pl.BlockSpec((None, t_pad, HEAD), lambda b, h, i: (6, b, h)),
                  pl.BlockSpec((None, 1, HEAD), lambda b, h, i: (layer, 0, 0)),
                  pl.BlockSpec(u.shape, lambda b, h, i: (0, 0))],
        out_specs=pl.BlockSpec((sub * CHUNK, HEAD), lambda b, h, i: (b * n_g + i, h)),
        out_shape=jax.ShapeDtypeStruct((n_seq * t_pad, width), BF16),
        scratch_shapes=[pltpu.VMEM((sub, CHUNK, CHUNK), F32), pltpu.VMEM((sub, CHUNK, HEAD), F32)],
        compiler_params=_params(("parallel", "parallel", "arbitrary")),
        name="sb_prompt",
    )(p7, p7, p7, gain, u)


def _sb_cache_sweep(q_bf, k_ref, v_ref, first, u_ref, r_ref, acc_ref, scale, row0):
    def cond(c):
        j, worst = c
        return jnp.logical_and(j >= 0, worst > SB_EXIT)

    def body(c):
        j, _ = c
        sl = pl.ds(pl.multiple_of(row0 + j * CHUNK, 8), CHUNK)
        _sb_tile(q_bf, k_ref[sl, :], v_ref[sl, :], None, u_ref, r_ref, acc_ref, scale)
        return j - 1, jnp.max(r_ref[...])

    lax.while_loop(cond, body, (jnp.int32(first), jnp.max(r_ref[...])))


def _sb_sample_kernel(q_ref, kn_ref, vn_ref, kc_ref, vc_ref, gain_ref, u_ref, y_ref,
                      r_ref, acc_ref, *, scale, rows, n_cache):
    q_bf = q_ref[...].astype(BF16)
    r_ref[...] = jnp.zeros_like(r_ref)
    acc_ref[...] = jnp.zeros_like(acc_ref)
    ti = lax.broadcasted_iota(jnp.int32, (rows, CHUNK), 0)
    si = lax.broadcasted_iota(jnp.int32, (rows, CHUNK), 1)
    pad = jnp.zeros((CHUNK - rows, HEAD), F32)
    _sb_tile(q_bf, jnp.concatenate([kn_ref[...], pad], axis=0), jnp.concatenate([vn_ref[...], pad], axis=0),
             si < ti, u_ref, r_ref, acc_ref, scale)
    n_full, rem = divmod(n_cache, CHUNK)
    _sb_cache_sweep(q_bf, kc_ref, vc_ref, n_full - 1, u_ref, r_ref, acc_ref, scale, rem)
    if rem:
        @pl.when(jnp.max(r_ref[...]) > SB_EXIT)
        def _():
            _sb_tile(q_bf, kc_ref[0:CHUNK, :], vc_ref[0:CHUNK, :], si < rem, u_ref, r_ref, acc_ref, scale)
    y_ref[...] = _rms_rows(acc_ref[...], gain_ref[...]).astype(BF16)


def _sb_sample(p7, cache_k, cache_v, gain, u, layer, *, n_seq, rows, row_blk0):
    _, n, width = p7.shape
    heads = width // HEAD
    n_cache = cache_k.shape[2]
    assert n_cache >= CHUNK and rows <= CHUNK
    kern = functools.partial(_sb_sample_kernel, scale=HEAD ** -0.5, rows=rows, n_cache=n_cache)
    new_spec = lambda seg: pl.BlockSpec((None, rows, HEAD), lambda b, h: (seg, row_blk0 + b, h))
    cache_spec = pl.BlockSpec((None, None, n_cache, HEAD), lambda b, h: (layer, b, 0, h))
    return pl.pallas_call(
        kern,
        grid=(n_seq, heads),
        in_specs=[new_spec(4), new_spec(5), new_spec(6), cache_spec, cache_spec,
                  pl.BlockSpec((None, 1, HEAD), lambda b, h: (layer, 0, 0)),
                  pl.BlockSpec(u.shape, lambda b, h: (0, 0))],
        out_specs=pl.BlockSpec((rows, HEAD), lambda b, h: (b, h)),
        out_shape=jax.ShapeDtypeStruct((n_seq * rows, width), BF16),
        scratch_shapes=[pltpu.VMEM((rows, CHUNK), F32), pltpu.VMEM((rows, HEAD), F32)],
        compiler_params=_params(("parallel", "parallel")),
        name="sb_sample",
    )(p7, p7, p7, cache_k, cache_v, gain, u)


def _outproj_kernel(yap_ref, ybp_ref, yas_ref, ybs_ref, wa_ref, wb_ref, x_ref, o_ref, *, prompt_tiles):
    def mix(ya_ref, yb_ref):
        o_ref[...] = x_ref[...] + _dot(ya_ref[...], wa_ref[...]) + _dot(yb_ref[...], wb_ref[...])

    is_prompt = pl.program_id(0) < prompt_tiles
    pl.when(is_prompt)(lambda: mix(yap_ref, ybp_ref))
    pl.when(jnp.logical_not(is_prompt))(lambda: mix(yas_ref, ybs_ref))


def _outproj(y_prompt, y_sample, w_bf, x, layer):
    n, d = x.shape
    width = y_prompt[0].shape[1]
    pt, st = y_prompt[0].shape[0] // TOKEN_TILE, y_sample[0].shape[0] // TOKEN_TILE
    assert pt * TOKEN_TILE == y_prompt[0].shape[0] and st * TOKEN_TILE == y_sample[0].shape[0] and pt + st == n // TOKEN_TILE
    p_spec = pl.BlockSpec((TOKEN_TILE, width), lambda i, j: (jnp.minimum(i, pt - 1), 0))
    s_spec = pl.BlockSpec((TOKEN_TILE, width), lambda i, j: (jnp.maximum(i - pt, 0), 0))
    return pl.pallas_call(
        functools.partial(_outproj_kernel, prompt_tiles=pt),
        grid=(n // TOKEN_TILE, d // COL_TILE),
        in_specs=[p_spec, p_spec, s_spec, s_spec,
                  pl.BlockSpec((None, None, width, COL_TILE), lambda i, j: (layer, 0, 0, j)),
                  pl.BlockSpec((None, None, width, COL_TILE), lambda i, j: (layer, 1, 0, j)),
                  pl.BlockSpec((TOKEN_TILE, COL_TILE), lambda i, j: (i, j))],
        out_specs=pl.BlockSpec((TOKEN_TILE, COL_TILE), lambda i, j: (i, j)),
        out_shape=jax.ShapeDtypeStruct((n, d), F32),
        input_output_aliases={6: 0},
        compiler_params=_params(("parallel", "arbitrary")),
        name="outproj",
    )(*y_prompt, *y_sample, w_bf, w_bf, x)


def _swiglu_step(h, w1_ref, w3_ref, w2_ref):
    a = _dot(h, w1_ref[...])
    b = _dot(h, w3_ref[...])
    g = (a / (1.0 + jnp.exp(-a))) * b
    return _dot(g.astype(BF16), w2_ref[...])


def _ffn_kernel(x_ref, g_ref, w1_ref, w3_ref, w2_ref, o_ref, h_ref):
    @pl.when(pl.program_id(1) == 0)
    def _():
        x = x_ref[...]
        h_ref[...] = _rms_rows(x, g_ref[...]).astype(BF16)
        o_ref[...] = x

    o_ref[...] += _swiglu_step(h_ref[...], w1_ref, w3_ref, w2_ref)


def _ffn(x, gain, w1, w3, w2, layer, idx):
    n, d = x.shape
    ff = w1.shape[-1]
    return pl.pallas_call(
        _ffn_kernel,
        grid=(n // TOKEN_TILE, ff // FF_TILE),
        in_specs=[pl.BlockSpec((TOKEN_TILE, d), lambda i, f: (i, 0)),
                  pl.BlockSpec((None, 1, d), lambda i, f: (layer, 0, 0)),
                  pl.BlockSpec((None, d, FF_TILE), lambda i, f: (idx, 0, f)),
                  pl.BlockSpec((None, d, FF_TILE), lambda i, f: (idx, 0, f)),
                  pl.BlockSpec((None, FF_TILE, d), lambda i, f: (idx, f, 0))],
        out_specs=pl.BlockSpec((TOKEN_TILE, d), lambda i, f: (i, 0)),
        out_shape=jax.ShapeDtypeStruct((n, d), F32),
        scratch_shapes=[pltpu.VMEM((TOKEN_TILE, d), BF16)],
        compiler_params=_params(("parallel", "arbitrary")),
        name="ffn_dense",
    )(x, gain, w1, w3, w2)


def _route(h, rt_ref, n_experts):
    h_hi, h_lo = _split2(h)
    r = rt_ref[...]
    r_hi, r_lo = _split2(r)
    logits = _dot(h_hi, r_hi) + _dot(h_lo, r_hi) + _dot(h_hi, r_lo)
    lane = lax.broadcasted_iota(jnp.int32, logits.shape, 1)
    live = lane < n_experts
    logits = jnp.where(live, logits, -jnp.inf)
    ex = jnp.exp(logits - jnp.max(logits, axis=-1, keepdims=True))
    p = ex / jnp.sum(ex, axis=-1, keepdims=True)
    p = jnp.where(live, p, -1.0)
    p1 = jnp.max(p, axis=-1, keepdims=True)
    i1 = jnp.min(jnp.where(p == p1, lane, HEAD), axis=-1, keepdims=True)
    rest = jnp.where(lane == i1, -1.0, p)
    p2 = jnp.max(rest, axis=-1, keepdims=True)
    i2 = jnp.min(jnp.where(rest == p2, lane, HEAD), axis=-1, keepdims=True)
    tot = p1 + p2
    g1, g2 = p1 / tot, p2 / tot
    return jnp.where(lane == i1, g1, 0.0) + jnp.where(lane == i2, g2, 0.0), i1, i2, g1, g2


def _router_kernel(x_ref, g_ref, rt_ref, o_ref, *, n_experts):
    gates, i1, i2, g1, g2 = _route(_rms_rows(x_ref[...], g_ref[...]), rt_ref, n_experts)
    lane = lax.broadcasted_iota(jnp.int32, gates.shape, 1)
    rec = jnp.where(lane == ROUTE_I1, i1.astype(F32), gates)
    rec = jnp.where(lane == ROUTE_I2, i2.astype(F32), rec)
    rec = jnp.where(lane == ROUTE_G1, g1, rec)
    o_ref[...] = jnp.where(lane == ROUTE_G2, g2, rec)


def _router(x, gain, router, layer, idx, n_experts):
    n, d = x.shape
    assert n_experts <= ROUTE_I1
    return pl.pallas_call(
        functools.partial(_router_kernel, n_experts=n_experts),
        grid=(n // TOKEN_TILE,),
        in_specs=[pl.BlockSpec((TOKEN_TILE, d), lambda i: (i, 0)),
                  pl.BlockSpec((None, 1, d), lambda i: (layer, 0, 0)),
                  pl.BlockSpec((None, d, HEAD), lambda i: (idx, 0, 0))],
        out_specs=pl.BlockSpec((TOKEN_TILE, HEAD), lambda i: (i, 0)),
        out_shape=jax.ShapeDtypeStruct((n, HEAD), F32),
        compiler_params=_params(("parallel",)),
        name="moe_router",
    )(x, gain, router)


def _dispatch_plan(route, n_experts):
    n = route.shape[0]
    n_pick = 2 * n
    n_tiles = n_pick // MOE_TILE + n_experts
    e_flat = route[:, ROUTE_I1:ROUTE_I2 + 1].astype(jnp.int32).reshape(n_pick)
    onehot = (e_flat[:, None] == jnp.arange(n_experts, dtype=jnp.int32)[None, :]).astype(jnp.int32)
    csum = jnp.cumsum(onehot, axis=0)
    counts = csum[-1]
    rank = jnp.sum(onehot * csum, axis=1) - 1
    tiles_per = (counts + MOE_TILE - 1) // MOE_TILE
    tile_end = jnp.cumsum(tiles_per)
    tile_start = tile_end - tiles_per
    dest = jnp.sum(onehot * tile_start[None, :], axis=1) * MOE_TILE + rank
    pick = jnp.zeros((n_tiles * MOE_TILE,), jnp.int32).at[dest].set(jnp.arange(n_pick, dtype=jnp.int32))
    t = jnp.arange(n_tiles, dtype=jnp.int32)
    live = t < tile_end[-1]
    t_eff = jnp.minimum(t, tile_end[-1] - 1)
    tile_expert = jnp.sum((t_eff[:, None] >= tile_end[None, :]).astype(jnp.int32), axis=1)
    used = (t_eff - tile_start[tile_expert]) * MOE_TILE
    tile_rows = jnp.where(live, jnp.clip(counts[tile_expert] - used, 0, MOE_TILE), 0)
    return tile_expert, tile_rows.astype(jnp.int32), pick


def _moe_kernel(te_ref, rows_ref, pick_ref, x_hbm, g_ref, w1_ref, w3_ref, w2_ref, y_hbm,
                xg_ref, h_ref, acc_ref, sem_in, sem_out, *, n_tokens):
    i = pl.program_id(0)
    f = pl.program_id(1)
    rows = rows_ref[i]
    base = i * MOE_TILE

    def row_in(p):
        tok = pick_ref[base + p] >> 1
        return pltpu.make_async_copy(x_hbm.at[pl.ds(tok, 1)], xg_ref.at[pl.ds(p, 1)], sem_in)

    def row_out(p):
        pick = pick_ref[base + p]
        dst = (pick & 1) * n_tokens + (pick >> 1)
        return pltpu.make_async_copy(acc_ref.at[pl.ds(p, 1)], y_hbm.at[pl.ds(dst, 1)], sem_out)

    @pl.when(rows > 0)
    def _():
        @pl.when(f == 0)
        def _():
            def start(p, c):
                row_in(p).start()
                return c

            lax.fori_loop(0, MOE_TILE, start, 0)

            def wait(p, c):
                row_in(p).wait()
                return c

            lax.fori_loop(0, MOE_TILE, wait, 0)
            h_ref[...] = _rms_rows(xg_ref[...], g_ref[...]).astype(BF16)
            acc_ref[...] = jnp.zeros_like(acc_ref)

        acc_ref[...] += _swiglu_step(h_ref[...], w1_ref, w3_ref, w2_ref)

        @pl.when(f == pl.num_programs(1) - 1)
        def _():
            def start(p, c):
                row_out(p).start()
                return c

            lax.fori_loop(0, rows, start, 0)

            def wait(p, c):
                row_out(p).wait()
                return c

            lax.fori_loop(0, rows, wait, 0)


def _moe_experts(x, gain, plan, w1, w3, w2, layer, idx):
    n, d = x.shape
    ff = w1.shape[-1]
    tile_expert, tile_rows, pick = plan
    n_tiles = tile_expert.shape[0]
    n_ff = ff // FF_TILE
    fblk = lambda i, f, rows: jnp.where(rows[i] > 0, f, n_ff - 1)
    grid_spec = pltpu.PrefetchScalarGridSpec(
        num_scalar_prefetch=3,
        grid=(n_tiles, n_ff),
        in_specs=[pl.BlockSpec(memory_space=pl.ANY),
                  pl.BlockSpec((None, 1, d), lambda i, f, te, rows, pk: (layer, 0, 0)),
                  pl.BlockSpec((None, None, d, FF_TILE), lambda i, f, te, rows, pk: (idx, te[i], 0, fblk(i, f, rows))),
                  pl.BlockSpec((None, None, d, FF_TILE), lambda i, f, te, rows, pk: (idx, te[i], 0, fblk(i, f, rows))),
                  pl.BlockSpec((None, None, FF_TILE, d), lambda i, f, te, rows, pk: (idx, te[i], fblk(i, f, rows), 0))],
        out_specs=pl.BlockSpec(memory_space=pl.ANY),
        scratch_shapes=[pltpu.VMEM((MOE_TILE, d), F32), pltpu.VMEM((MOE_TILE, d), BF16),
                        pltpu.VMEM((MOE_TILE, d), F32), pltpu.SemaphoreType.DMA, pltpu.SemaphoreType.DMA],
    )
    return pl.pallas_call(
        functools.partial(_moe_kernel, n_tokens=n),
        grid_spec=grid_spec,
        out_shape=jax.ShapeDtypeStruct((2 * n, d), F32),
        compiler_params=_params(("arbitrary", "arbitrary")),
        name="moe_experts",
    )(tile_expert, tile_rows, pick, x, gain, w1, w3, w2)


def _moe_combine_kernel(x_ref, y1_ref, y2_ref, r_ref, o_ref):
    r = r_ref[...]
    o_ref[...] = (x_ref[...] + r[:, ROUTE_G1:ROUTE_G1 + 1] * y1_ref[...]
                  + r[:, ROUTE_G2:ROUTE_G2 + 1] * y2_ref[...])


def _moe_combine(x, y, route):
    n, d = x.shape
    nt = n // TOKEN_TILE
    return pl.pallas_call(
        _moe_combine_kernel,
        grid=(nt,),
        in_specs=[pl.BlockSpec((TOKEN_TILE, d), lambda i: (i, 0)),
                  pl.BlockSpec((TOKEN_TILE, d), lambda i: (i, 0)),
                  pl.BlockSpec((TOKEN_TILE, d), lambda i: (nt + i, 0)),
                  pl.BlockSpec((TOKEN_TILE, HEAD), lambda i: (i, 0))],
        out_specs=pl.BlockSpec((TOKEN_TILE, d), lambda i: (i, 0)),
        out_shape=jax.ShapeDtypeStruct((n, d), F32),
        input_output_aliases={0: 0},
        compiler_params=_params(("parallel",)),
        name="moe_combine",
    )(x, y, y, route)


def _moe(x, gain, router, w1, w3, w2, layer, idx, n_experts):
    route = _router(x, gain, router, layer, idx, n_experts)
    y = _moe_experts(x, gain, _dispatch_plan(route, n_experts), w1, w3, w2, layer, idx)
    return _moe_combine(x, y, route)


def _final_norm_kernel(x_ref, g_ref, o_ref):
    o_ref[...] = _rms_rows(x_ref[...], g_ref[...])


def _final_norm(x, gain):
    n, d = x.shape
    return pl.pallas_call(
        _final_norm_kernel,
        grid=(n // TOKEN_TILE,),
        in_specs=[pl.BlockSpec((TOKEN_TILE, d), lambda i: (i, 0)),
                  pl.BlockSpec((1, d), lambda i: (0, 0))],
        out_specs=pl.BlockSpec((TOKEN_TILE, d), lambda i: (i, 0)),
        out_shape=jax.ShapeDtypeStruct((n, d), F32),
        compiler_params=_params(("parallel",)),
        name="final_norm",
    )(x, gain)


def _hgrn_rows(t_pad):
    n = t_pad // CHUNK
    best = max(c for c in range(1, n + 1) if n % c == 0 and c * CHUNK <= 2048)
    return best * CHUNK


def kernel(x_prompt, x_sample, cache_k, cache_v, state_hgrn, meta_tokens, mix_norm, w_in, lb_logits, hgrn_norm, sb_norm, w_out, ffn_norm, dense_w1, dense_w3, dense_w2, moe_router, moe_w1, moe_w3, moe_w2, final_norm):
    n_seq, seq, d = x_prompt.shape
    dec_b, dec_s, _ = x_sample.shape
    depth = w_in.shape[0]
    n_meta = meta_tokens.shape[0]
    heads = cache_k.shape[3]
    width = heads * HEAD
    n_cache = cache_k.shape[2]
    t_real = n_meta + seq
    n_sample = dec_b * dec_s
    t_pad = next(t for t in range(_round_up(t_real, CHUNK), t_real + TOKEN_TILE * CHUNK, CHUNK)
                 if (n_seq * t + n_sample) % TOKEN_TILE == 0)
    n_prompt = n_seq * t_pad
    assert w_in.shape[2] == N_SEG * width and n_prompt % dec_s == 0 and dec_s % 8 == 0

    meta = jnp.broadcast_to(meta_tokens.astype(F32)[None], (n_seq, n_meta, d))
    slab = jnp.concatenate([meta, x_prompt, jnp.zeros((n_seq, t_pad - t_real, d), F32)], axis=1)
    x = jnp.concatenate([slab.reshape(n_prompt, d), x_sample.reshape(n_sample, d)], axis=0)

    p_lb = jax.nn.softmax(lb_logits.astype(F32), axis=0)
    lb_all = (jnp.cumsum(p_lb, axis=0) - p_lb[0]).reshape(depth, 1, width)

    w_in_bf = w_in.astype(BF16)
    w_out_bf = w_out.astype(BF16).reshape(depth, 2, width, d)
    dense_bf = [w.astype(BF16) for w in (dense_w1, dense_w3, dense_w2)]
    moe_bf = [w.astype(BF16) for w in (moe_w1, moe_w3, moe_w2)]
    n_experts = moe_router.shape[-1]
    router = jnp.pad(moe_router.astype(F32), ((0, 0), (0, 0), (0, HEAD - n_experts)))
    mix_g = mix_norm.reshape(depth, 1, d)
    ffn_g = ffn_norm.reshape(depth, 1, d)
    hgrn_g = hgrn_norm.reshape(depth, 1, HEAD)
    sb_g = sb_norm.reshape(depth, 1, HEAD)
    hgrn_m = jnp.asarray(_hgrn_masks(), BF16)
    sb_u = jnp.asarray(_sb_masks(), BF16)
    cache_k2 = cache_k.reshape(depth, dec_b, n_cache, width)
    cache_v2 = cache_v.reshape(depth, dec_b, n_cache, width)
    rows_p = _hgrn_rows(t_pad)

    kp, vp, sp, ks, vs, ss = [], [], [], [], [], []
    for l in range(depth):
        p7 = _inproj(x, mix_g, w_in_bf, l, width)
        ya_p, s_p = _hgrn(p7, lb_all, hgrn_g, hgrn_m, l, n_seq=n_seq, rows=rows_p, n_blk=t_pad // rows_p,
                          row_blk0=0, t_valid=t_real)
        ya_s, s_s = _hgrn(p7, lb_all, hgrn_g, hgrn_m, l, n_seq=dec_b, rows=dec_s, n_blk=1,
                          row_blk0=n_prompt // dec_s, t_valid=dec_s, s0=state_hgrn)
        yb_p = _sb_prompt(p7, sb_g, sb_u, l, n_seq=n_seq, t_pad=t_pad)
        yb_s = _sb_sample(p7, cache_k2, cache_v2, sb_g, sb_u, l, n_seq=dec_b, rows=dec_s,
                          row_blk0=n_prompt // dec_s)
        x = _outproj((ya_p, yb_p), (ya_s, yb_s), w_out_bf, x, l)
        if l % 2 == 0:
            x = _ffn(x, ffn_g, *dense_bf, l, l // 2)
        else:
            x = _moe(x, ffn_g, router, *moe_bf, l, l // 2, n_experts)

        kv = lambda seg: p7[seg, :n_prompt].reshape(n_seq, t_pad, heads, HEAD)[:, :t_real]
        kv_s = lambda seg: p7[seg, n_prompt:n_prompt + n_sample].reshape(dec_b, dec_s, heads, HEAD)
        kp.append(kv(5))
        vp.append(kv(6))
        sp.append(s_p)
        ks.append(kv_s(5))
        vs.append(kv_s(6))
        ss.append(s_s)

    y = _final_norm(x, final_norm.reshape(1, d))
    y_prompt = y[:n_prompt].reshape(n_seq, t_pad, d)[:, n_meta:t_real]
    y_sample = y[n_prompt:n_prompt + n_sample].reshape(dec_b, dec_s, d)
    return (y_prompt, y_sample, jnp.stack(kp), jnp.stack(vp), jnp.stack(sp),
            jnp.stack(ks), jnp.stack(vs), jnp.stack(ss))
```

```python
import functools

import numpy as np
import jax
import jax.numpy as jnp
from jax import lax
from jax.experimental import pallas as pl
from jax.experimental.pallas import tpu as pltpu

F32 = jnp.float32
BF16 = jnp.bfloat16
EPS = 1e-6
LB_FLOOR = 1e-20
HEAD = 128
CHUNK = 128
LEVELS = 7
N_SEG = 7
TOKEN_TILE = 512
COL_TILE = 512
IN_COL_TILE = 1024
MOE_TILE = 512
FF_TILE = 512
VMEM_LIMIT = 56 * 1024 * 1024
HGRN_GROUP = 4
SB_GROUP = 6
SB_EXIT = -104.0
ROUTE_I1, ROUTE_I2, ROUTE_G1, ROUTE_G2 = 8, 9, 10, 11


def _round_up(x, m):
    return (x + m - 1) // m * m


def _params(sem):
    return pltpu.CompilerParams(dimension_semantics=sem, vmem_limit_bytes=VMEM_LIMIT)


def _dot(a, b):
    return jnp.dot(a, b, preferred_element_type=F32)


def _dot_nt(a, b):
    return lax.dot_general(a, b, (((1,), (1,)), ((), ())), preferred_element_type=F32)


def _split2(x):
    hi = x.astype(BF16)
    lo = (x - hi.astype(F32)).astype(BF16)
    return hi, lo


def _rms_rows(x, g):
    return x * lax.rsqrt(jnp.mean(x * x, axis=-1, keepdims=True) + EPS) * g


def _inproj_kernel(x_ref, g_ref, w_ref, o_ref, h_ref):
    @pl.when(pl.program_id(1) == 0)
    def _():
        h_ref[...] = _rms_rows(x_ref[...], g_ref[...]).astype(BF16)

    o_ref[...] = _dot(h_ref[...], w_ref[...])


def _inproj(x, gain, w_bf, layer, width):
    n, d = x.shape
    per_seg = width // IN_COL_TILE
    grid = (n // TOKEN_TILE, N_SEG * per_seg)
    return pl.pallas_call(
        _inproj_kernel,
        grid=grid,
        in_specs=[
            pl.BlockSpec((TOKEN_TILE, d), lambda i, j: (i, 0)),
            pl.BlockSpec((None, 1, d), lambda i, j: (layer, 0, 0)),
            pl.BlockSpec((None, d, IN_COL_TILE), lambda i, j: (layer, 0, j)),
        ],
        out_specs=pl.BlockSpec((None, TOKEN_TILE, IN_COL_TILE), lambda i, j: (j // per_seg, i, j % per_seg)),
        out_shape=jax.ShapeDtypeStruct((N_SEG, n, width), F32),
        scratch_shapes=[pltpu.VMEM((TOKEN_TILE, d), BF16)],
        compiler_params=_params(("parallel", "arbitrary")),
        name="inproj",
    )(x, gain, w_bf)


def _hgrn_masks():
    c = CHUNK
    t = np.arange(c)[:, None]
    j = np.arange(c)[None, :]
    blocks = [(j <= t)]
    for l in range(LEVELS):
        hs = 1 << l
        mid = (t >> (l + 1) << (l + 1)) + hs
        upper = ((t >> l) & 1) == 1
        blocks.append(np.where(upper, (j >= mid) & (j <= t), (j > t) & (j <= mid - 1)))
    blocks.append(j > t)
    return np.concatenate(blocks, axis=0).astype(np.float32)


def _hgrn_chunk(q, fa, v, ga, valid, lbc, gain, m_ref, st_ref, group):
    log_lb, log1m_lb, one_m_lb = lbc
    e = jnp.exp(-jnp.abs(fa))
    log_sig = jnp.minimum(fa, 0.0) - jnp.log1p(e)
    bterm = log1m_lb + log_sig
    lf = jnp.maximum(log_lb, bterm) + jnp.log1p(jnp.exp(-jnp.abs(log_lb - bterm)))
    k = one_m_lb * (jnp.where(fa > 0.0, e, 1.0) / (1.0 + e))
    lf = jnp.where(valid, lf, 0.0)
    k = jnp.where(valid, k, 0.0)

    lf_hi, lf_lo = _split2(lf)
    m = m_ref[...]
    dall = _dot(m, lf_hi) + _dot(m, lf_lo)
    b = dall[0:CHUNK]
    d_last = dall[(LEVELS + 1) * CHUNK:(LEVELS + 2) * CHUNK]

    heads = [slice(hh * HEAD, (hh + 1) * HEAD) for hh in range(group)]
    row = lax.broadcasted_iota(jnp.int32, q.shape, 0)
    ti = lax.broadcasted_iota(jnp.int32, (CHUNK, CHUNK), 0)
    si = lax.broadcasted_iota(jnp.int32, (CHUNK, CHUNK), 1)
    top_bit = 31 - lax.clz(ti ^ si)
    owner = jnp.where(si < ti, top_bit, jnp.where(si == ti, -1, -2))

    q_bf = q.astype(BF16)
    k_bf = k.astype(BF16)
    a = [jnp.where(owner == -1, _dot_nt(q_bf[:, hd], k_bf[:, hd]), 0.0) for hd in heads]
    for l in range(LEVELS):
        upper = ((row >> l) & 1) == 1
        z = (jnp.where(upper, q, k) * jnp.exp(dall[(l + 1) * CHUNK:(l + 2) * CHUNK])).astype(BF16)
        a = [jnp.where(owner == l, _dot_nt(z[:, hd], z[:, hd]), a[hh]) for hh, hd in enumerate(heads)]

    v_bf = v.astype(BF16)
    qe = (q * jnp.exp(b)).astype(BF16)
    w = (k * jnp.exp(d_last)).astype(BF16)
    keep = jnp.exp(b[CHUNK - 1:CHUNK, :])
    st = [st_ref[hh] for hh in range(group)]
    o = [_dot(a[hh].astype(BF16), v_bf[:, hd]) + _dot_nt(qe[:, hd], st[hh].astype(BF16))
         for hh, hd in enumerate(heads)]
    for hh, hd in enumerate(heads):
        st_ref[hh] = st[hh] * keep[:, hd] + _dot(v[:, hd].T.astype(BF16), w[:, hd])

    y = jnp.concatenate([_rms_rows(oh, gain) for oh in o], axis=1)
    return (y * (ga / (1.0 + jnp.exp(-ga)))).astype(BF16)


def _hgrn_kernel(*refs, rows, t_valid, group, has_s0):
    q_ref, f_ref, i_ref, g_ref, lb_ref, gain_ref, m_ref = refs[:7]
    s0_ref = refs[7] if has_s0 else None
    y_ref, s_ref, st_ref = refs[-3:]
    r = pl.program_id(2)

    @pl.when(r == 0)
    def _():
        for hh in range(group):
            st_ref[hh] = s0_ref[hh].T if has_s0 else jnp.zeros((HEAD, HEAD), F32)

    gain = gain_ref[...]
    lb = lb_ref[...]
    lbc = (jnp.log(jnp.maximum(lb, LB_FLOOR)), jnp.log1p(-lb), 1.0 - lb)
    row = lax.broadcasted_iota(jnp.int32, (CHUNK, group * HEAD), 0)

    if rows < CHUNK:
        pad = jnp.zeros((CHUNK - rows, group * HEAD), F32)
        ld = lambda ref: jnp.concatenate([ref[...], pad], axis=0)
        y = _hgrn_chunk(ld(q_ref), ld(f_ref), ld(i_ref), ld(g_ref), row < t_valid, lbc, gain, m_ref, st_ref, group)
        y_ref[...] = y[:rows]
    else:
        def body(c, carry):
            sl = pl.ds(pl.multiple_of(c * CHUNK, CHUNK), CHUNK)
            valid = (r * rows + c * CHUNK + row) < t_valid
            y_ref[sl, :] = _hgrn_chunk(q_ref[sl, :], f_ref[sl, :], i_ref[sl, :], g_ref[sl, :],
                                       valid, lbc, gain, m_ref, st_ref, group)
            return carry

        lax.fori_loop(0, rows // CHUNK, body, 0)

    @pl.when(r == pl.num_programs(2) - 1)
    def _():
        for hh in range(group):
            s_ref[hh] = st_ref[hh].T


def _hgrn(p7, lb, gain, masks, layer, *, n_seq, rows, n_blk, row_blk0, t_valid, s0=None):
    _, n, width = p7.shape
    heads = width // HEAD
    group = HGRN_GROUP
    gw = group * HEAD
    seg_spec = lambda seg: pl.BlockSpec((None, rows, gw),
                                        lambda b, h, r: (seg, row_blk0 + b * n_blk + r, h))
    in_specs = [seg_spec(0), seg_spec(1), seg_spec(2), seg_spec(3),
                pl.BlockSpec((None, 1, gw), lambda b, h, r: (layer, 0, h)),
                pl.BlockSpec((None, 1, HEAD), lambda b, h, r: (layer, 0, 0)),
                pl.BlockSpec(masks.shape, lambda b, h, r: (0, 0))]
    args = [p7, p7, p7, p7, lb, gain, masks]
    if s0 is not None:
        in_specs.append(pl.BlockSpec((None, None, group, HEAD, HEAD), lambda b, h, r: (layer, b, h, 0, 0)))
        args.append(s0)
    kern = functools.partial(_hgrn_kernel, rows=rows, t_valid=t_valid, group=group, has_s0=s0 is not None)
    return pl.pallas_call(
        kern,
        grid=(n_seq, heads // group, n_blk),
        in_specs=in_specs,
        out_specs=[pl.BlockSpec((rows, gw), lambda b, h, r: (b * n_blk + r, h)),
                   pl.BlockSpec((None, group, HEAD, HEAD), lambda b, h, r: (b, h, 0, 0))],
        out_shape=[jax.ShapeDtypeStruct((n_seq * n_blk * rows, width), BF16),
                   jax.ShapeDtypeStruct((n_seq, heads, HEAD, HEAD), F32)],
        scratch_shapes=[pltpu.VMEM((group, HEAD, HEAD), F32)],
        compiler_params=_params(("parallel", "parallel", "arbitrary")),
        name="hgrn_sample" if s0 is not None else "hgrn_prompt",
    )(*args)


def _sb_masks():
    j = np.arange(CHUNK)[:, None]
    s = np.arange(CHUNK)[None, :]
    return np.concatenate([np.ones((CHUNK, CHUNK)), (j > s)], axis=1).astype(np.float32)


def _sb_tiles(q_bf, k_t, v_t, mask, u_ref, r_ref, acc_ref, scale):
    g, tq, _ = q_bf.shape
    z = lax.dot_general(q_bf, k_t.astype(BF16), (((2,), (2,)), ((0,), (0,))), preferred_element_type=F32) * scale
    log_beta = jnp.minimum(z, 0.0) - jnp.log1p(jnp.exp(-jnp.abs(z)))
    log_rest = log_beta - z
    if mask is not None:
        log_rest = jnp.where(mask, log_rest, 0.0)
    hi, lo = _split2(log_rest.reshape(g * tq, CHUNK))
    u = u_ref[...]
    c = (_dot(hi, u) + _dot(lo, u)).reshape(g, tq, 2 * CHUNK)
    r = r_ref[...]
    w = jnp.exp(log_beta + r + c[:, :, CHUNK:])
    if mask is not None:
        w = jnp.where(mask, w, 0.0)
    acc_ref[...] += lax.dot_general(w.astype(BF16), v_t.astype(BF16), (((2,), (1,)), ((0,), (0,))),
                                    preferred_element_type=F32)
    r_ref[...] = r + c[:, :, :CHUNK]


def _sb_prompt_kernel(q_ref, k_ref, v_ref, gain_ref, u_ref, y_ref, r_ref, acc_ref, *, scale, sub):
    first = pl.program_id(2) * sub
    r_ref[...] = jnp.zeros_like(r_ref)
    acc_ref[...] = jnp.zeros_like(acc_ref)
    ti = lax.broadcasted_iota(jnp.int32, (sub, CHUNK, CHUNK), 1)
    si = lax.broadcasted_iota(jnp.int32, (sub, CHUNK, CHUNK), 2)
    ai = lax.broadcasted_iota(jnp.int32, (sub, CHUNK, CHUNK), 0)
    q_bf = q_ref[...].astype(BF16).reshape(sub, CHUNK, HEAD)

    def tiles(step, mask):
        rows = [pl.ds(pl.multiple_of(jnp.maximum(first + a - step, 0) * CHUNK, CHUNK), CHUNK) for a in range(sub)]
        k_t = jnp.stack([k_ref[sl, :] for sl in rows])
        v_t = jnp.stack([v_ref[sl, :] for sl in rows])
        _sb_tiles(q_bf, k_t, v_t, mask, u_ref, r_ref, acc_ref, scale)

    tiles(0, si < ti)

    def cond(c):
        step, worst = c
        return jnp.logical_and(step < first + sub, worst > SB_EXIT)

    def body(c):
        step, _ = c
        tiles(step, first + ai - step >= 0)
        return step + 1, jnp.max(r_ref[...])

    lax.while_loop(cond, body, (jnp.int32(1), jnp.max(r_ref[...])))
    y = _rms_rows(acc_ref[...], gain_ref[...]).astype(BF16)
    y_ref[...] = y.reshape(sub * CHUNK, HEAD)


def _sb_prompt(p7, gain, u, layer, *, n_seq, t_pad):
    _, n, width = p7.shape
    heads = width // HEAD
    n_q = t_pad // CHUNK
    sub = max(c for c in range(1, SB_GROUP + 1) if n_q % c == 0)
    n_g = n_q // sub
    kern = functools.partial(_sb_prompt_kernel, scale=HEAD ** -0.5, sub=sub)
    return pl.pallas_call(
        kern,
        grid=(n_seq, heads, n_g),
        in_specs=[pl.BlockSpec((None, sub * CHUNK, HEAD), lambda b, h, i: (4, b * n_g + i, h)),
                  pl.BlockSpec((None, t_pad, HEAD), lambda b, h, i: (5, b, h)),
                  pl.BlockSpec((None, t_pad, HEAD), lambda b, h, i: (6, b, h)),
                  pl.BlockSpec((None, 1, HEAD), lambda b, h, i: (layer, 0, 0)),
                  pl.BlockSpec(u.shape, lambda b, h, i: (0, 0))],
        out_specs=pl.BlockSpec((sub * CHUNK, HEAD), lambda b, h, i: (b * n_g + i, h)),
        out_shape=jax.ShapeDtypeStruct((n_seq * t_pad, width), BF16),
        scratch_shapes=[pltpu.VMEM((sub, CHUNK, CHUNK), F32), pltpu.VMEM((sub, CHUNK, HEAD), F32)],
        compiler_params=_params(("parallel", "parallel", "arbitrary")),
        name="sb_prompt",
    )(p7, p7, p7, gain, u)


def _sb_cache_sweep(q_bf, k_ref, v_ref, first, u_ref, r_ref, acc_ref, scale, row0):
    def cond(c):
        j, worst = c
        return jnp.logical_and(j >= 0, worst > SB_EXIT)

    def body(c):
        j, _ = c
        sl = pl.ds(pl.multiple_of(row0 + j * CHUNK, 8), CHUNK)
        _sb_tiles(q_bf, k_ref[sl, :][None], v_ref[sl, :][None], None, u_ref, r_ref, acc_ref, scale)
        return j - 1, jnp.max(r_ref[...])

    lax.while_loop(cond, body, (jnp.int32(first), jnp.max(r_ref[...])))


def _sb_sample_kernel(q_ref, kn_ref, vn_ref, kc_ref, vc_ref, gain_ref, u_ref, y_ref,
                      r_ref, acc_ref, *, scale, rows, n_cache):
    q_bf = q_ref[...].astype(BF16)[None]
    r_ref[...] = jnp.zeros_like(r_ref)
    acc_ref[...] = jnp.zeros_like(acc_ref)
    ti = lax.broadcasted_iota(jnp.int32, (1, rows, CHUNK), 1)
    si = lax.broadcasted_iota(jnp.int32, (1, rows, CHUNK), 2)
    pad = jnp.zeros((CHUNK - rows, HEAD), F32)
    _sb_tiles(q_bf, jnp.concatenate([kn_ref[...], pad], axis=0)[None], jnp.concatenate([vn_ref[...], pad], axis=0)[None],
              si < ti, u_ref, r_ref, acc_ref, scale)
    n_full, rem = divmod(n_cache, CHUNK)
    _sb_cache_sweep(q_bf, kc_ref, vc_ref, n_full - 1, u_ref, r_ref, acc_ref, scale, rem)
    if rem:
        @pl.when(jnp.max(r_ref[...]) > SB_EXIT)
        def _():
            _sb_tiles(q_bf, kc_ref[0:CHUNK, :][None], vc_ref[0:CHUNK, :][None], si < rem, u_ref, r_ref, acc_ref, scale)
    y_ref[...] = _rms_rows(acc_ref[0], gain_ref[...]).astype(BF16)


def _sb_sample(p7, cache_k, cache_v, gain, u, layer, *, n_seq, rows, row_blk0):
    _, n, width = p7.shape
    heads = width // HEAD
    n_cache = cache_k.shape[2]
    assert n_cache >= CHUNK and rows <= CHUNK
    kern = functools.partial(_sb_sample_kernel, scale=HEAD ** -0.5, rows=rows, n_cache=n_cache)
    new_spec = lambda seg: pl.BlockSpec((None, rows, HEAD), lambda b, h: (seg, row_blk0 + b, h))
    cache_spec = pl.BlockSpec((None, None, n_cache, HEAD), lambda b, h: (layer, b, 0, h))
    return pl.pallas_call(
        kern,
        grid=(n_seq, heads),
        in_specs=[new_spec(4), new_spec(5), new_spec(6), cache_spec, cache_spec,
                  pl.BlockSpec((None, 1, HEAD), lambda b, h: (layer, 0, 0)),
                  pl.BlockSpec(u.shape, lambda b, h: (0, 0))],
        out_specs=pl.BlockSpec((rows, HEAD), lambda b, h: (b, h)),
        out_shape=jax.ShapeDtypeStruct((n_seq * rows, width), BF16),
        scratch_shapes=[pltpu.VMEM((1, rows, CHUNK), F32), pltpu.VMEM((1, rows, HEAD), F32)],
        compiler_params=_params(("parallel", "parallel")),
        name="sb_sample",
    )(p7, p7, p7, cache_k, cache_v, gain, u)


def _outproj_kernel(yap_ref, ybp_ref, yas_ref, ybs_ref, wa_ref, wb_ref, x_ref, o_ref, *, prompt_tiles):
    def mix(ya_ref, yb_ref):
        o_ref[...] = x_ref[...] + _dot(ya_ref[...], wa_ref[...]) + _dot(yb_ref[...], wb_ref[...])

    is_prompt = pl.program_id(0) < prompt_tiles
    pl.when(is_prompt)(lambda: mix(yap_ref, ybp_ref))
    pl.when(jnp.logical_not(is_prompt))(lambda: mix(yas_ref, ybs_ref))


def _outproj(y_prompt, y_sample, w_bf, x, layer):
    n, d = x.shape
    width = y_prompt[0].shape[1]
    pt, st = y_prompt[0].shape[0] // TOKEN_TILE, y_sample[0].shape[0] // TOKEN_TILE
    assert pt * TOKEN_TILE == y_prompt[0].shape[0] and st * TOKEN_TILE == y_sample[0].shape[0] and pt + st == n // TOKEN_TILE
    p_spec = pl.BlockSpec((TOKEN_TILE, width), lambda i, j: (jnp.minimum(i, pt - 1), 0))
    s_spec = pl.BlockSpec((TOKEN_TILE, width), lambda i, j: (jnp.maximum(i - pt, 0), 0))
    return pl.pallas_call(
        functools.partial(_outproj_kernel, prompt_tiles=pt),
        grid=(n // TOKEN_TILE, d // COL_TILE),
        in_specs=[p_spec, p_spec, s_spec, s_spec,
                  pl.BlockSpec((None, None, width, COL_TILE), lambda i, j: (layer, 0, 0, j)),
                  pl.BlockSpec((None, None, width, COL_TILE), lambda i, j: (layer, 1, 0, j)),
                  pl.BlockSpec((TOKEN_TILE, COL_TILE), lambda i, j: (i, j))],
        out_specs=pl.BlockSpec((TOKEN_TILE, COL_TILE), lambda i, j: (i, j)),
        out_shape=jax.ShapeDtypeStruct((n, d), F32),
        input_output_aliases={6: 0},
        compiler_params=_params(("parallel", "arbitrary")),
        name="outproj",
    )(*y_prompt, *y_sample, w_bf, w_bf, x)


def _swiglu_step(h, w1_ref, w3_ref, w2_ref):
    a = _dot(h, w1_ref[...])
    b = _dot(h, w3_ref[...])
    g = (a / (1.0 + jnp.exp(-a))) * b
    return _dot(g.astype(BF16), w2_ref[...])


def _ffn_kernel(x_ref, g_ref, w1_ref, w3_ref, w2_ref, o_ref, h_ref):
    @pl.when(pl.program_id(1) == 0)
    def _():
        x = x_ref[...]
        h_ref[...] = _rms_rows(x, g_ref[...]).astype(BF16)
        o_ref[...] = x

    o_ref[...] += _swiglu_step(h_ref[...], w1_ref, w3_ref, w2_ref)


def _ffn(x, gain, w1, w3, w2, layer, idx):
    n, d = x.shape
    ff = w1.shape[-1]
    return pl.pallas_call(
        _ffn_kernel,
        grid=(n // TOKEN_TILE, ff // FF_TILE),
        in_specs=[pl.BlockSpec((TOKEN_TILE, d), lambda i, f: (i, 0)),
                  pl.BlockSpec((None, 1, d), lambda i, f: (layer, 0, 0)),
                  pl.BlockSpec((None, d, FF_TILE), lambda i, f: (idx, 0, f)),
                  pl.BlockSpec((None, d, FF_TILE), lambda i, f: (idx, 0, f)),
                  pl.BlockSpec((None, FF_TILE, d), lambda i, f: (idx, f, 0))],
        out_specs=pl.BlockSpec((TOKEN_TILE, d), lambda i, f: (i, 0)),
        out_shape=jax.ShapeDtypeStruct((n, d), F32),
        scratch_shapes=[pltpu.VMEM((TOKEN_TILE, d), BF16)],
        compiler_params=_params(("parallel", "arbitrary")),
        name="ffn_dense",
    )(x, gain, w1, w3, w2)


def _route(h, rt_ref, n_experts):
    h_hi, h_lo = _split2(h)
    r = rt_ref[...]
    r_hi, r_lo = _split2(r)
    logits = _dot(h_hi, r_hi) + _dot(h_lo, r_hi) + _dot(h_hi, r_lo)
    lane = lax.broadcasted_iota(jnp.int32, logits.shape, 1)
    live = lane < n_experts
    logits = jnp.where(live, logits, -jnp.inf)
    ex = jnp.exp(logits - jnp.max(logits, axis=-1, keepdims=True))
    p = ex / jnp.sum(ex, axis=-1, keepdims=True)
    p = jnp.where(live, p, -1.0)
    p1 = jnp.max(p, axis=-1, keepdims=True)
    i1 = jnp.min(jnp.where(p == p1, lane, HEAD), axis=-1, keepdims=True)
    rest = jnp.where(lane == i1, -1.0, p)
    p2 = jnp.max(rest, axis=-1, keepdims=True)
    i2 = jnp.min(jnp.where(rest == p2, lane, HEAD), axis=-1, keepdims=True)
    tot = p1 + p2
    g1, g2 = p1 / tot, p2 / tot
    return jnp.where(lane == i1, g1, 0.0) + jnp.where(lane == i2, g2, 0.0), i1, i2, g1, g2


def _router_kernel(x_ref, g_ref, rt_ref, o_ref, *, n_experts):
    gates, i1, i2, g1, g2 = _route(_rms_rows(x_ref[...], g_ref[...]), rt_ref, n_experts)
    lane = lax.broadcasted_iota(jnp.int32, gates.shape, 1)
    rec = jnp.where(lane == ROUTE_I1, i1.astype(F32), gates)
    rec = jnp.where(lane == ROUTE_I2, i2.astype(F32), rec)
    rec = jnp.where(lane == ROUTE_G1, g1, rec)
    o_ref[...] = jnp.where(lane == ROUTE_G2, g2, rec)


def _router(x, gain, router, layer, idx, n_experts):
    n, d = x.shape
    assert n_experts <= ROUTE_I1
    return pl.pallas_call(
        functools.partial(_router_kernel, n_experts=n_experts),
        grid=(n // TOKEN_TILE,),
        in_specs=[pl.BlockSpec((TOKEN_TILE, d), lambda i: (i, 0)),
                  pl.BlockSpec((None, 1, d), lambda i: (layer, 0, 0)),
                  pl.BlockSpec((None, d, HEAD), lambda i: (idx, 0, 0))],
        out_specs=pl.BlockSpec((TOKEN_TILE, HEAD), lambda i: (i, 0)),
        out_shape=jax.ShapeDtypeStruct((n, HEAD), F32),
        compiler_params=_params(("parallel",)),
        name="moe_router",
    )(x, gain, router)


def _dispatch_plan(route, n_experts):
    n = route.shape[0]
    n_pick = 2 * n
    n_tiles = n_pick // MOE_TILE + n_experts
    e_flat = route[:, ROUTE_I1:ROUTE_I2 + 1].astype(jnp.int32).reshape(n_pick)
    onehot = (e_flat[:, None] == jnp.arange(n_experts, dtype=jnp.int32)[None, :]).astype(jnp.int32)
    csum = jnp.cumsum(onehot, axis=0)
    counts = csum[-1]
    rank = jnp.sum(onehot * csum, axis=1) - 1
    tiles_per = (counts + MOE_TILE - 1) // MOE_TILE
    tile_end = jnp.cumsum(tiles_per)
    tile_start = tile_end - tiles_per
    dest = jnp.sum(onehot * tile_start[None, :], axis=1) * MOE_TILE + rank
    pick = jnp.zeros((n_tiles * MOE_TILE,), jnp.int32).at[dest].set(jnp.arange(n_pick, dtype=jnp.int32))
    t = jnp.arange(n_tiles, dtype=jnp.int32)
    live = t < tile_end[-1]
    t_eff = jnp.minimum(t, tile_end[-1] - 1)
    tile_expert = jnp.sum((t_eff[:, None] >= tile_end[None, :]).astype(jnp.int32), axis=1)
    used = (t_eff - tile_start[tile_expert]) * MOE_TILE
    tile_rows = jnp.where(live, jnp.clip(counts[tile_expert] - used, 0, MOE_TILE), 0)
    return tile_expert, tile_rows.astype(jnp.int32), pick


def _moe_kernel(te_ref, rows_ref, pick_ref, x_hbm, g_ref, w1_ref, w3_ref, w2_ref, y_hbm,
                xg_ref, h_ref, acc_ref, yo_ref, sem_in, sem_out, *, n_tokens):
    i = pl.program_id(0)
    f = pl.program_id(1)
    n_tiles = pl.num_programs(0)
    n_ff = pl.num_programs(1)
    rows = rows_ref[i]
    slot = i % 2
    nxt = jnp.minimum(i + 1, n_tiles - 1)
    next_live = jnp.logical_and(i + 1 < n_tiles, rows_ref[nxt] > 0)

    def row_in(tile, p, to):
        tok = pick_ref[tile * MOE_TILE + p] >> 1
        return pltpu.make_async_copy(x_hbm.at[pl.ds(tok, 1)], xg_ref.at[to, pl.ds(p, 1)], sem_in.at[to])

    def row_out(tile, p):
        pick = pick_ref[tile * MOE_TILE + p]
        dst = (pick & 1) * n_tokens + (pick >> 1)
        return pltpu.make_async_copy(yo_ref.at[pl.ds(p, 1)], y_hbm.at[pl.ds(dst, 1)], sem_out)

    def each(lo, hi, fn):
        def body(p, c):
            fn(p)
            return c

        lax.fori_loop(lo, hi, body, 0)

    @pl.when(jnp.logical_and(i == 0, f == 0))
    def _():
        each(0, MOE_TILE, lambda p: row_in(0, p, 0).start())

    @pl.when(next_live)
    def _():
        per_step = -(-MOE_TILE // n_ff)
        lo = f * per_step
        each(lo, jnp.minimum(lo + per_step, MOE_TILE), lambda p: row_in(nxt, p, 1 - slot).start())

    @pl.when(rows > 0)
    def _():
        @pl.when(f == 0)
        def _():
            each(0, MOE_TILE, lambda p: row_in(i, p, slot).wait())
            h_ref[...] = _rms_rows(xg_ref[slot], g_ref[...]).astype(BF16)
            acc_ref[...] = jnp.zeros_like(acc_ref)

        y = _swiglu_step(h_ref[...], w1_ref, w3_ref, w2_ref)

        @pl.when(f < n_ff - 1)
        def _():
            acc_ref[...] += y

        @pl.when(f == n_ff - 1)
        def _():
            @pl.when(i > 0)
            def _():
                each(0, rows_ref[jnp.maximum(i - 1, 0)], lambda p: row_out(i - 1, p).wait())

            yo_ref[...] = acc_ref[...] + y
            each(0, rows, lambda p: row_out(i, p).start())

            @pl.when(jnp.logical_not(next_live))
            def _():
                each(0, rows, lambda p: row_out(i, p).wait())


def _moe_experts(x, gain, plan, w1, w3, w2, layer, idx):
    n, d = x.shape
    ff = w1.shape[-1]
    tile_expert, tile_rows, pick = plan
    n_tiles = tile_expert.shape[0]
    n_ff = ff // FF_TILE
    fblk = lambda i, f, rows: jnp.where(rows[i] > 0, f, n_ff - 1)
    grid_spec = pltpu.PrefetchScalarGridSpec(
        num_scalar_prefetch=3,
        grid=(n_tiles, n_ff),
        in_specs=[pl.BlockSpec(memory_space=pl.ANY),
                  pl.BlockSpec((None, 1, d), lambda i, f, te, rows, pk: (layer, 0, 0)),
                  pl.BlockSpec((None, None, d, FF_TILE), lambda i, f, te, rows, pk: (idx, te[i], 0, fblk(i, f, rows))),
                  pl.BlockSpec((None, None, d, FF_TILE), lambda i, f, te, rows, pk: (idx, te[i], 0, fblk(i, f, rows))),
                  pl.BlockSpec((None, None, FF_TILE, d), lambda i, f, te, rows, pk: (idx, te[i], fblk(i, f, rows), 0))],
        out_specs=pl.BlockSpec(memory_space=pl.ANY),
        scratch_shapes=[pltpu.VMEM((2, MOE_TILE, d), F32), pltpu.VMEM((MOE_TILE, d), BF16),
                        pltpu.VMEM((MOE_TILE, d), F32), pltpu.VMEM((MOE_TILE, d), F32),
                        pltpu.SemaphoreType.DMA((2,)), pltpu.SemaphoreType.DMA],
    )
    return pl.pallas_call(
        functools.partial(_moe_kernel, n_tokens=n),
        grid_spec=grid_spec,
        out_shape=jax.ShapeDtypeStruct((2 * n, d), F32),
        compiler_params=_params(("arbitrary", "arbitrary")),
        name="moe_experts",
    )(tile_expert, tile_rows, pick, x, gain, w1, w3, w2)


def _moe_combine_kernel(x_ref, y1_ref, y2_ref, r_ref, o_ref):
    r = r_ref[...]
    o_ref[...] = (x_ref[...] + r[:, ROUTE_G1:ROUTE_G1 + 1] * y1_ref[...]
                  + r[:, ROUTE_G2:ROUTE_G2 + 1] * y2_ref[...])


def _moe_combine(x, y, route):
    n, d = x.shape
    nt = n // TOKEN_TILE
    return pl.pallas_call(
        _moe_combine_kernel,
        grid=(nt,),
        in_specs=[pl.BlockSpec((TOKEN_TILE, d), lambda i: (i, 0)),
                  pl.BlockSpec((TOKEN_TILE, d), lambda i: (i, 0)),
                  pl.BlockSpec((TOKEN_TILE, d), lambda i: (nt + i, 0)),
                  pl.BlockSpec((TOKEN_TILE, HEAD), lambda i: (i, 0))],
        out_specs=pl.BlockSpec((TOKEN_TILE, d), lambda i: (i, 0)),
        out_shape=jax.ShapeDtypeStruct((n, d), F32),
        input_output_aliases={0: 0},
        compiler_params=_params(("parallel",)),
        name="moe_combine",
    )(x, y, y, route)


def _moe(x, gain, router, w1, w3, w2, layer, idx, n_experts):
    route = _router(x, gain, router, layer, idx, n_experts)
    y = _moe_experts(x, gain, _dispatch_plan(route, n_experts), w1, w3, w2, layer, idx)
    return _moe_combine(x, y, route)


def _final_norm_kernel(x_ref, g_ref, o_ref):
    o_ref[...] = _rms_rows(x_ref[...], g_ref[...])


def _final_norm(x, gain):
    n, d = x.shape
    return pl.pallas_call(
        _final_norm_kernel,
        grid=(n // TOKEN_TILE,),
        in_specs=[pl.BlockSpec((TOKEN_TILE, d), lambda i: (i, 0)),
                  pl.BlockSpec((1, d), lambda i: (0, 0))],
        out_specs=pl.BlockSpec((TOKEN_TILE, d), lambda i: (i, 0)),
        out_shape=jax.ShapeDtypeStruct((n, d), F32),
        compiler_params=_params(("parallel",)),
        name="final_norm",
    )(x, gain)


def _hgrn_rows(t_pad):
    n = t_pad // CHUNK
    best = max(c for c in range(1, n + 1) if n % c == 0 and c * CHUNK <= 2048)
    return best * CHUNK


def kernel(x_prompt, x_sample, cache_k, cache_v, state_hgrn, meta_tokens, mix_norm, w_in, lb_logits, hgrn_norm, sb_norm, w_out, ffn_norm, dense_w1, dense_w3, dense_w2, moe_router, moe_w1, moe_w3, moe_w2, final_norm):
    n_seq, seq, d = x_prompt.shape
    dec_b, dec_s, _ = x_sample.shape
    depth = w_in.shape[0]
    n_meta = meta_tokens.shape[0]
    heads = cache_k.shape[3]
    width = heads * HEAD
    n_cache = cache_k.shape[2]
    t_real = n_meta + seq
    n_sample = dec_b * dec_s
    t_pad = next(t for t in range(_round_up(t_real, CHUNK), t_real + TOKEN_TILE * CHUNK, CHUNK)
                 if (n_seq * t + n_sample) % TOKEN_TILE == 0)
    n_prompt = n_seq * t_pad
    assert w_in.shape[2] == N_SEG * width and n_prompt % dec_s == 0 and dec_s % 8 == 0

    meta = jnp.broadcast_to(meta_tokens.astype(F32)[None], (n_seq, n_meta, d))
    slab = jnp.concatenate([meta, x_prompt, jnp.zeros((n_seq, t_pad - t_real, d), F32)], axis=1)
    x = jnp.concatenate([slab.reshape(n_prompt, d), x_sample.reshape(n_sample, d)], axis=0)

    p_lb = jax.nn.softmax(lb_logits.astype(F32), axis=0)
    lb_all = (jnp.cumsum(p_lb, axis=0) - p_lb[0]).reshape(depth, 1, width)

    w_in_bf = w_in.astype(BF16)
    w_out_bf = w_out.astype(BF16).reshape(depth, 2, width, d)
    dense_bf = [w.astype(BF16) for w in (dense_w1, dense_w3, dense_w2)]
    moe_bf = [w.astype(BF16) for w in (moe_w1, moe_w3, moe_w2)]
    n_experts = moe_router.shape[-1]
    router = jnp.pad(moe_router.astype(F32), ((0, 0), (0, 0), (0, HEAD - n_experts)))
    mix_g = mix_norm.reshape(depth, 1, d)
    ffn_g = ffn_norm.reshape(depth, 1, d)
    hgrn_g = hgrn_norm.reshape(depth, 1, HEAD)
    sb_g = sb_norm.reshape(depth, 1, HEAD)
    hgrn_m = jnp.asarray(_hgrn_masks(), BF16)
    sb_u = jnp.asarray(_sb_masks(), BF16)
    cache_k2 = cache_k.reshape(depth, dec_b, n_cache, width)
    cache_v2 = cache_v.reshape(depth, dec_b, n_cache, width)
    rows_p = _hgrn_rows(t_pad)

    kp, vp, sp, ks, vs, ss = [], [], [], [], [], []
    for l in range(depth):
        p7 = _inproj(x, mix_g, w_in_bf, l, width)
        ya_p, s_p = _hgrn(p7, lb_all, hgrn_g, hgrn_m, l, n_seq=n_seq, rows=rows_p, n_blk=t_pad // rows_p,
                          row_blk0=0, t_valid=t_real)
        ya_s, s_s = _hgrn(p7, lb_all, hgrn_g, hgrn_m, l, n_seq=dec_b, rows=dec_s, n_blk=1,
                          row_blk0=n_prompt // dec_s, t_valid=dec_s, s0=state_hgrn)
        yb_p = _sb_prompt(p7, sb_g, sb_u, l, n_seq=n_seq, t_pad=t_pad)
        yb_s = _sb_sample(p7, cache_k2, cache_v2, sb_g, sb_u, l, n_seq=dec_b, rows=dec_s,
                          row_blk0=n_prompt // dec_s)
        x = _outproj((ya_p, yb_p), (ya_s, yb_s), w_out_bf, x, l)
        if l % 2 == 0:
            x = _ffn(x, ffn_g, *dense_bf, l, l // 2)
        else:
            x = _moe(x, ffn_g, router, *moe_bf, l, l // 2, n_experts)

        kv = lambda seg: p7[seg, :n_prompt].reshape(n_seq, t_pad, heads, HEAD)[:, :t_real]
        kv_s = lambda seg: p7[seg, n_prompt:n_prompt + n_sample].reshape(dec_b, dec_s, heads, HEAD)
        kp.append(kv(5))
        vp.append(kv(6))
        sp.append(s_p)
        ks.append(kv_s(5))
        vs.append(kv_s(6))
        ss.append(s_s)

    y = _final_norm(x, final_norm.reshape(1, d))
    y_prompt = y[:n_prompt].reshape(n_seq, t_pad, d)[:, n_meta:t_real]
    y_sample = y[n_prompt:n_prompt + n_sample].reshape(dec_b, dec_s, d)
    return (y_prompt, y_sample, jnp.stack(kp), jnp.stack(vp), jnp.stack(sp),
            jnp.stack(ks), jnp.stack(vs), jnp.stack(ss))
```

```python
import functools

import numpy as np
import jax
import jax.numpy as jnp
from jax import lax
from jax.experimental import pallas as pl
from jax.experimental.pallas import tpu as pltpu

F32 = jnp.float32
BF16 = jnp.bfloat16
EPS = 1e-6
LB_FLOOR = 1e-20
HEAD = 128
CHUNK = 128
LEVELS = 7
N_SEG = 7
TOKEN_TILE = 512
COL_TILE = 512
IN_COL_TILE = 1024
MOE_TILE = 512
FF_TILE = 512
VMEM_LIMIT = 56 * 1024 * 1024
HGRN_GROUP = 4
SB_GROUP = 6
SB_EXIT = -104.0
ROUTE_I1, ROUTE_I2, ROUTE_G1, ROUTE_G2 = 8, 9, 10, 11


def _round_up(x, m):
    return (x + m - 1) // m * m


def _params(sem):
    return pltpu.CompilerParams(dimension_semantics=sem, vmem_limit_bytes=VMEM_LIMIT)


def _dot(a, b):
    return jnp.dot(a, b, preferred_element_type=F32)


def _dot_nt(a, b):
    return lax.dot_general(a, b, (((1,), (1,)), ((), ())), preferred_element_type=F32)


def _split2(x):
    hi = x.astype(BF16)
    lo = (x - hi.astype(F32)).astype(BF16)
    return hi, lo


def _rms_rows(x, g):
    return x * lax.rsqrt(jnp.mean(x * x, axis=-1, keepdims=True) + EPS) * g


def _inproj_kernel(x_ref, g_ref, w_ref, o_ref, h_ref):
    @pl.when(pl.program_id(1) == 0)
    def _():
        h_ref[...] = _rms_rows(x_ref[...], g_ref[...]).astype(BF16)

    o_ref[...] = _dot(h_ref[...], w_ref[...])


def _inproj(x, gain, w_bf, layer, width):
    n, d = x.shape
    per_seg = width // IN_COL_TILE
    grid = (n // TOKEN_TILE, N_SEG * per_seg)
    return pl.pallas_call(
        _inproj_kernel,
        grid=grid,
        in_specs=[
            pl.BlockSpec((TOKEN_TILE, d), lambda i, j: (i, 0)),
            pl.BlockSpec((None, 1, d), lambda i, j: (layer, 0, 0)),
            pl.BlockSpec((None, d, IN_COL_TILE), lambda i, j: (layer, 0, j)),
        ],
        out_specs=pl.BlockSpec((None, TOKEN_TILE, IN_COL_TILE), lambda i, j: (j // per_seg, i, j % per_seg)),
        out_shape=jax.ShapeDtypeStruct((N_SEG, n, width), F32),
        scratch_shapes=[pltpu.VMEM((TOKEN_TILE, d), BF16)],
        compiler_params=_params(("parallel", "arbitrary")),
        name="inproj",
    )(x, gain, w_bf)


def _hgrn_masks():
    c = CHUNK
    t = np.arange(c)[:, None]
    j = np.arange(c)[None, :]
    blocks = [(j <= t)]
    for l in range(LEVELS):
        hs = 1 << l
        mid = (t >> (l + 1) << (l + 1)) + hs
        upper = ((t >> l) & 1) == 1
        blocks.append(np.where(upper, (j >= mid) & (j <= t), (j > t) & (j <= mid - 1)))
    blocks.append(j > t)
    return np.concatenate(blocks, axis=0).astype(np.float32)


def _hgrn_chunk(q, fa, v, ga, valid, lbc, gain, m_ref, st_ref, group):
    log_lb, log1m_lb, one_m_lb = lbc
    e = jnp.exp(-jnp.abs(fa))
    log_sig = jnp.minimum(fa, 0.0) - jnp.log1p(e)
    bterm = log1m_lb + log_sig
    lf = jnp.maximum(log_lb, bterm) + jnp.log1p(jnp.exp(-jnp.abs(log_lb - bterm)))
    k = one_m_lb * (jnp.where(fa > 0.0, e, 1.0) / (1.0 + e))
    lf = jnp.where(valid, lf, 0.0)
    k = jnp.where(valid, k, 0.0)

    lf_hi, lf_lo = _split2(lf)
    m = m_ref[...]
    dall = _dot(m, lf_hi) + _dot(m, lf_lo)
    b = dall[0:CHUNK]
    d_last = dall[(LEVELS + 1) * CHUNK:(LEVELS + 2) * CHUNK]

    heads = [slice(hh * HEAD, (hh + 1) * HEAD) for hh in range(group)]
    row = lax.broadcasted_iota(jnp.int32, q.shape, 0)
    ti = lax.broadcasted_iota(jnp.int32, (CHUNK, CHUNK), 0)
    si = lax.broadcasted_iota(jnp.int32, (CHUNK, CHUNK), 1)
    top_bit = 31 - lax.clz(ti ^ si)
    owner = jnp.where(si < ti, top_bit, jnp.where(si == ti, -1, -2))

    q_bf = q.astype(BF16)
    k_bf = k.astype(BF16)
    a = [jnp.where(owner == -1, _dot_nt(q_bf[:, hd], k_bf[:, hd]), 0.0) for hd in heads]
    for l in range(LEVELS):
        upper = ((row >> l) & 1) == 1
        z = (jnp.where(upper, q, k) * jnp.exp(dall[(l + 1) * CHUNK:(l + 2) * CHUNK])).astype(BF16)
        a = [jnp.where(owner == l, _dot_nt(z[:, hd], z[:, hd]), a[hh]) for hh, hd in enumerate(heads)]

    v_bf = v.astype(BF16)
    qe = (q * jnp.exp(b)).astype(BF16)
    w = (k * jnp.exp(d_last)).astype(BF16)
    keep = jnp.exp(b[CHUNK - 1:CHUNK, :])
    st = [st_ref[hh] for hh in range(group)]
    o = [_dot(a[hh].astype(BF16), v_bf[:, hd]) + _dot_nt(qe[:, hd], st[hh].astype(BF16))
         for hh, hd in enumerate(heads)]
    for hh, hd in enumerate(heads):
        st_ref[hh] = st[hh] * keep[:, hd] + _dot(v[:, hd].T.astype(BF16), w[:, hd])

    y = jnp.concatenate([_rms_rows(oh, gain) for oh in o], axis=1)
    return (y * (ga / (1.0 + jnp.exp(-ga)))).astype(BF16)


def _hgrn_kernel(*refs, rows, t_valid, group, has_s0):
    q_ref, f_ref, i_ref, g_ref, lb_ref, gain_ref, m_ref = refs[:7]
    s0_ref = refs[7] if has_s0 else None
    y_ref, s_ref, st_ref = refs[-3:]
    r = pl.program_id(2)

    @pl.when(r == 0)
    def _():
        for hh in range(group):
            st_ref[hh] = s0_ref[hh].T if has_s0 else jnp.zeros((HEAD, HEAD), F32)

    gain = gain_ref[...]
    lb = lb_ref[...]
    lbc = (jnp.log(jnp.maximum(lb, LB_FLOOR)), jnp.log1p(-lb), 1.0 - lb)
    row = lax.broadcasted_iota(jnp.int32, (CHUNK, group * HEAD), 0)

    if rows < CHUNK:
        pad = jnp.zeros((CHUNK - rows, group * HEAD), F32)
        ld = lambda ref: jnp.concatenate([ref[...], pad], axis=0)
        y = _hgrn_chunk(ld(q_ref), ld(f_ref), ld(i_ref), ld(g_ref), row < t_valid, lbc, gain, m_ref, st_ref, group)
        y_ref[...] = y[:rows]
    else:
        def body(c, carry):
            sl = pl.ds(pl.multiple_of(c * CHUNK, CHUNK), CHUNK)
            valid = (r * rows + c * CHUNK + row) < t_valid
            y_ref[sl, :] = _hgrn_chunk(q_ref[sl, :], f_ref[sl, :], i_ref[sl, :], g_ref[sl, :],
                                       valid, lbc, gain, m_ref, st_ref, group)
            return carry

        lax.fori_loop(0, rows // CHUNK, body, 0)

    @pl.when(r == pl.num_programs(2) - 1)
    def _():
        for hh in range(group):
            s_ref[hh] = st_ref[hh].T


def _hgrn(p7, lb, gain, masks, layer, *, n_seq, rows, n_blk, row_blk0, t_valid, s0=None):
    _, n, width = p7.shape
    heads = width // HEAD
    group = HGRN_GROUP
    gw = group * HEAD
    seg_spec = lambda seg: pl.BlockSpec((None, rows, gw),
                                        lambda b, h, r: (seg, row_blk0 + b * n_blk + r, h))
    in_specs = [seg_spec(0), seg_spec(1), seg_spec(2), seg_spec(3),
                pl.BlockSpec((None, 1, gw), lambda b, h, r: (layer, 0, h)),
                pl.BlockSpec((None, 1, HEAD), lambda b, h, r: (layer, 0, 0)),
                pl.BlockSpec(masks.shape, lambda b, h, r: (0, 0))]
    args = [p7, p7, p7, p7, lb, gain, masks]
    if s0 is not None:
        in_specs.append(pl.BlockSpec((None, None, group, HEAD, HEAD), lambda b, h, r: (layer, b, h, 0, 0)))
        args.append(s0)
    kern = functools.partial(_hgrn_kernel, rows=rows, t_valid=t_valid, group=group, has_s0=s0 is not None)
    return pl.pallas_call(
        kern,
        grid=(n_seq, heads // group, n_blk),
        in_specs=in_specs,
        out_specs=[pl.BlockSpec((rows, gw), lambda b, h, r: (b * n_blk + r, h)),
                   pl.BlockSpec((None, group, HEAD, HEAD), lambda b, h, r: (b, h, 0, 0))],
        out_shape=[jax.ShapeDtypeStruct((n_seq * n_blk * rows, width), BF16),
                   jax.ShapeDtypeStruct((n_seq, heads, HEAD, HEAD), F32)],
        scratch_shapes=[pltpu.VMEM((group, HEAD, HEAD), F32)],
        compiler_params=_params(("parallel", "parallel", "arbitrary")),
        name="hgrn_sample" if s0 is not None else "hgrn_prompt",
    )(*args)


def _sb_masks():
    j = np.arange(CHUNK)[:, None]
    s = np.arange(CHUNK)[None, :]
    return np.concatenate([np.ones((CHUNK, CHUNK)), (j > s)], axis=1).astype(np.float32)


def _sb_tiles(q_bf, k_t, v_t, mask, u_ref, r_ref, acc_ref, scale):
    g, tq, _ = q_bf.shape
    z = lax.dot_general(q_bf, k_t.astype(BF16), (((2,), (2,)), ((0,), (0,))), preferred_element_type=F32) * scale
    log_beta = jnp.minimum(z, 0.0) - jnp.log1p(jnp.exp(-jnp.abs(z)))
    log_rest = log_beta - z
    if mask is not None:
        log_rest = jnp.where(mask, log_rest, 0.0)
    hi, lo = _split2(log_rest.reshape(g * tq, CHUNK))
    u = u_ref[...]
    c = (_dot(hi, u) + _dot(lo, u)).reshape(g, tq, 2 * CHUNK)
    r = r_ref[...]
    w = jnp.exp(log_beta + r + c[:, :, CHUNK:])
    if mask is not None:
        w = jnp.where(mask, w, 0.0)
    acc_ref[...] += lax.dot_general(w.astype(BF16), v_t.astype(BF16), (((2,), (1,)), ((0,), (0,))),
                                    preferred_element_type=F32)
    r_ref[...] = r + c[:, :, :CHUNK]


def _sb_prompt_kernel(q_ref, k_ref, v_ref, gain_ref, u_ref, y_ref, r_ref, acc_ref, *, scale, sub):
    first = pl.program_id(2) * sub
    r_ref[...] = jnp.zeros_like(r_ref)
    acc_ref[...] = jnp.zeros_like(acc_ref)
    ti = lax.broadcasted_iota(jnp.int32, (sub, CHUNK, CHUNK), 1)
    si = lax.broadcasted_iota(jnp.int32, (sub, CHUNK, CHUNK), 2)
    ai = lax.broadcasted_iota(jnp.int32, (sub, CHUNK, CHUNK), 0)
    q_bf = q_ref[...].astype(BF16).reshape(sub, CHUNK, HEAD)

    def tiles(step, mask):
        rows = [pl.ds(pl.multiple_of(jnp.maximum(first + a - step, 0) * CHUNK, CHUNK), CHUNK) for a in range(sub)]
        k_t = jnp.stack([k_ref[sl, :] for sl in rows])
        v_t = jnp.stack([v_ref[sl, :] for sl in rows])
        _sb_tiles(q_bf, k_t, v_t, mask, u_ref, r_ref, acc_ref, scale)

    tiles(0, si < ti)

    def cond(c):
        step, worst = c
        return jnp.logical_and(step < first + sub, worst > SB_EXIT)

    def body(c):
        step, _ = c
        tiles(step, first + ai - step >= 0)
        return step + 1, jnp.max(r_ref[...])

    lax.while_loop(cond, body, (jnp.int32(1), jnp.max(r_ref[...])))
    y = _rms_rows(acc_ref[...], gain_ref[...]).astype(BF16)
    y_ref[...] = y.reshape(sub * CHUNK, HEAD)


def _sb_prompt(p7, gain, u, layer, *, n_seq, t_pad):
    _, n, width = p7.shape
    heads = width // HEAD
    n_q = t_pad // CHUNK
    sub = max(c for c in range(1, SB_GROUP + 1) if n_q % c == 0)
    n_g = n_q // sub
    kern = functools.partial(_sb_prompt_kernel, scale=HEAD ** -0.5, sub=sub)
    return pl.pallas_call(
        kern,
        grid=(n_seq, heads, n_g),
        in_specs=[pl.BlockSpec((None, sub * CHUNK, HEAD), lambda b, h, i: (4, b * n_g + i, h)),
                  pl.BlockSpec((None, t_pad, HEAD), lambda b, h, i: (5, b, h)),
                  pl.BlockSpec((None, t_pad, HEAD), lambda b, h, i: (6, b, h)),
                  pl.BlockSpec((None, 1, HEAD), lambda b, h, i: (layer, 0, 0)),
                  pl.BlockSpec(u.shape, lambda b, h, i: (0, 0))],
        out_specs=pl.BlockSpec((sub * CHUNK, HEAD), lambda b, h, i: (b * n_g + i, h)),
        out_shape=jax.ShapeDtypeStruct((n_seq * t_pad, width), BF16),
        scratch_shapes=[pltpu.VMEM((sub, CHUNK, CHUNK), F32), pltpu.VMEM((sub, CHUNK, HEAD), F32)],
        compiler_params=_params(("parallel", "parallel", "arbitrary")),
        name="sb_prompt",
    )(p7, p7, p7, gain, u)


def _sb_sample_kernel(q_ref, kn_ref, vn_ref, kc_ref, vc_ref, gain_ref, u_ref, y_ref,
                      qb_ref, r_ref, acc_ref, *, scale, rows, heads, n_cache):
    kt = pl.program_id(1)
    n_kt = pl.num_programs(1)
    cols = [slice(h * HEAD, (h + 1) * HEAD) for h in range(heads)]
    ti = lax.broadcasted_iota(jnp.int32, (heads, rows, CHUNK), 1)
    si = lax.broadcasted_iota(jnp.int32, (heads, rows, CHUNK), 2)

    @pl.when(kt == 0)
    def _():
        qb_ref[...] = jnp.stack([q_ref[:, c] for c in cols]).astype(BF16)
        r_ref[...] = jnp.zeros_like(r_ref)
        acc_ref[...] = jnp.zeros_like(acc_ref)
        pad = jnp.zeros((CHUNK - rows, HEAD), F32)
        k_t = jnp.stack([jnp.concatenate([kn_ref[:, c], pad], axis=0) for c in cols])
        v_t = jnp.stack([jnp.concatenate([vn_ref[:, c], pad], axis=0) for c in cols])
        _sb_tiles(qb_ref[...], k_t, v_t, si < ti, u_ref, r_ref, acc_ref, scale)

    @pl.when(jnp.max(r_ref[...]) > SB_EXIT)
    def _():
        left = n_cache - (n_kt - 1 - kt) * CHUNK
        live = lax.broadcasted_iota(jnp.int32, (heads, CHUNK, HEAD), 1) < left
        k_t = jnp.where(live, jnp.stack([kc_ref[:, h, :] for h in range(heads)]), 0.0)
        v_t = jnp.where(live, jnp.stack([vc_ref[:, h, :] for h in range(heads)]), 0.0)
        _sb_tiles(qb_ref[...], k_t, v_t, si < left, u_ref, r_ref, acc_ref, scale)

    @pl.when(kt == n_kt - 1)
    def _():
        gain = gain_ref[...]
        y_ref[...] = jnp.concatenate([_rms_rows(acc_ref[h], gain) for h in range(heads)], axis=1).astype(BF16)


def _sb_sample(p7, cache_k, cache_v, gain, u, layer, *, n_seq, rows, row_blk0):
    _, n, width = p7.shape
    n_cache, heads = cache_k.shape[2:4]
    assert rows <= CHUNK and width == heads * HEAD
    n_kt = pl.cdiv(n_cache, CHUNK)
    kern = functools.partial(_sb_sample_kernel, scale=HEAD ** -0.5, rows=rows, heads=heads, n_cache=n_cache)
    new_spec = lambda seg: pl.BlockSpec((None, rows, width), lambda b, t: (seg, row_blk0 + b, 0))
    cache_spec = pl.BlockSpec((None, None, CHUNK, heads, HEAD), lambda b, t: (layer, b, n_kt - 1 - t, 0, 0))
    return pl.pallas_call(
        kern,
        grid=(n_seq, n_kt),
        in_specs=[new_spec(4), new_spec(5), new_spec(6), cache_spec, cache_spec,
                  pl.BlockSpec((None, 1, HEAD), lambda b, t: (layer, 0, 0)),
                  pl.BlockSpec(u.shape, lambda b, t: (0, 0))],
        out_specs=pl.BlockSpec((rows, width), lambda b, t: (b, 0)),
        out_shape=jax.ShapeDtypeStruct((n_seq * rows, width), BF16),
        scratch_shapes=[pltpu.VMEM((heads, rows, HEAD), BF16), pltpu.VMEM((heads, rows, CHUNK), F32),
                        pltpu.VMEM((heads, rows, HEAD), F32)],
        compiler_params=_params(("parallel", "arbitrary")),
        name="sb_sample",
    )(p7, p7, p7, cache_k, cache_v, gain, u)


def _outproj_kernel(yap_ref, ybp_ref, yas_ref, ybs_ref, wa_ref, wb_ref, x_ref, o_ref, *, prompt_tiles):
    def mix(ya_ref, yb_ref):
        o_ref[...] = x_ref[...] + _dot(ya_ref[...], wa_ref[...]) + _dot(yb_ref[...], wb_ref[...])

    is_prompt = pl.program_id(0) < prompt_tiles
    pl.when(is_prompt)(lambda: mix(yap_ref, ybp_ref))
    pl.when(jnp.logical_not(is_prompt))(lambda: mix(yas_ref, ybs_ref))


def _outproj(y_prompt, y_sample, w_bf, x, layer):
    n, d = x.shape
    width = y_prompt[0].shape[1]
    pt, st = y_prompt[0].shape[0] // TOKEN_TILE, y_sample[0].shape[0] // TOKEN_TILE
    assert pt * TOKEN_TILE == y_prompt[0].shape[0] and st * TOKEN_TILE == y_sample[0].shape[0] and pt + st == n // TOKEN_TILE
    p_spec = pl.BlockSpec((TOKEN_TILE, width), lambda i, j: (jnp.minimum(i, pt - 1), 0))
    s_spec = pl.BlockSpec((TOKEN_TILE, width), lambda i, j: (jnp.maximum(i - pt, 0), 0))
    return pl.pallas_call(
        functools.partial(_outproj_kernel, prompt_tiles=pt),
        grid=(n // TOKEN_TILE, d // COL_TILE),
        in_specs=[p_spec, p_spec, s_spec, s_spec,
                  pl.BlockSpec((None, None, width, COL_TILE), lambda i, j: (layer, 0, 0, j)),
                  pl.BlockSpec((None, None, width, COL_TILE), lambda i, j: (layer, 1, 0, j)),
                  pl.BlockSpec((TOKEN_TILE, COL_TILE), lambda i, j: (i, j))],
        out_specs=pl.BlockSpec((TOKEN_TILE, COL_TILE), lambda i, j: (i, j)),
        out_shape=jax.ShapeDtypeStruct((n, d), F32),
        input_output_aliases={6: 0},
        compiler_params=_params(("parallel", "arbitrary")),
        name="outproj",
    )(*y_prompt, *y_sample, w_bf, w_bf, x)


def _swiglu_step(h, w1_ref, w3_ref, w2_ref):
    a = _dot(h, w1_ref[...])
    b = _dot(h, w3_ref[...])
    g = (a / (1.0 + jnp.exp(-a))) * b
    return _dot(g.astype(BF16), w2_ref[...])


def _ffn_kernel(x_ref, g_ref, w1_ref, w3_ref, w2_ref, o_ref, h_ref):
    @pl.when(pl.program_id(1) == 0)
    def _():
        x = x_ref[...]
        h_ref[...] = _rms_rows(x, g_ref[...]).astype(BF16)
        o_ref[...] = x

    o_ref[...] += _swiglu_step(h_ref[...], w1_ref, w3_ref, w2_ref)


def _ffn(x, gain, w1, w3, w2, layer, idx):
    n, d = x.shape
    ff = w1.shape[-1]
    return pl.pallas_call(
        _ffn_kernel,
        grid=(n // TOKEN_TILE, ff // FF_TILE),
        in_specs=[pl.BlockSpec((TOKEN_TILE, d), lambda i, f: (i, 0)),
                  pl.BlockSpec((None, 1, d), lambda i, f: (layer, 0, 0)),
                  pl.BlockSpec((None, d, FF_TILE), lambda i, f: (idx, 0, f)),
                  pl.BlockSpec((None, d, FF_TILE), lambda i, f: (idx, 0, f)),
                  pl.BlockSpec((None, FF_TILE, d), lambda i, f: (idx, f, 0))],
        out_specs=pl.BlockSpec((TOKEN_TILE, d), lambda i, f: (i, 0)),
        out_shape=jax.ShapeDtypeStruct((n, d), F32),
        scratch_shapes=[pltpu.VMEM((TOKEN_TILE, d), BF16)],
        compiler_params=_params(("parallel", "arbitrary")),
        name="ffn_dense",
    )(x, gain, w1, w3, w2)


def _route(h, rt_ref, n_experts):
    h_hi, h_lo = _split2(h)
    r = rt_ref[...]
    r_hi, r_lo = _split2(r)
    logits = _dot(h_hi, r_hi) + _dot(h_lo, r_hi) + _dot(h_hi, r_lo)
    lane = lax.broadcasted_iota(jnp.int32, logits.shape, 1)
    live = lane < n_experts
    logits = jnp.where(live, logits, -jnp.inf)
    ex = jnp.exp(logits - jnp.max(logits, axis=-1, keepdims=True))
    p = ex / jnp.sum(ex, axis=-1, keepdims=True)
    p = jnp.where(live, p, -1.0)
    p1 = jnp.max(p, axis=-1, keepdims=True)
    i1 = jnp.min(jnp.where(p == p1, lane, HEAD), axis=-1, keepdims=True)
    rest = jnp.where(lane == i1, -1.0, p)
    p2 = jnp.max(rest, axis=-1, keepdims=True)
    i2 = jnp.min(jnp.where(rest == p2, lane, HEAD), axis=-1, keepdims=True)
    tot = p1 + p2
    g1, g2 = p1 / tot, p2 / tot
    return jnp.where(lane == i1, g1, 0.0) + jnp.where(lane == i2, g2, 0.0), i1, i2, g1, g2


def _router_kernel(x_ref, g_ref, rt_ref, o_ref, *, n_experts):
    gates, i1, i2, g1, g2 = _route(_rms_rows(x_ref[...], g_ref[...]), rt_ref, n_experts)
    lane = lax.broadcasted_iota(jnp.int32, gates.shape, 1)
    rec = jnp.where(lane == ROUTE_I1, i1.astype(F32), gates)
    rec = jnp.where(lane == ROUTE_I2, i2.astype(F32), rec)
    rec = jnp.where(lane == ROUTE_G1, g1, rec)
    o_ref[...] = jnp.where(lane == ROUTE_G2, g2, rec)


def _router(x, gain, router, layer, idx, n_experts):
    n, d = x.shape
    assert n_experts <= ROUTE_I1
    return pl.pallas_call(
        functools.partial(_router_kernel, n_experts=n_experts),
        grid=(n // TOKEN_TILE,),
        in_specs=[pl.BlockSpec((TOKEN_TILE, d), lambda i: (i, 0)),
                  pl.BlockSpec((None, 1, d), lambda i: (layer, 0, 0)),
                  pl.BlockSpec((None, d, HEAD), lambda i: (idx, 0, 0))],
        out_specs=pl.BlockSpec((TOKEN_TILE, HEAD), lambda i: (i, 0)),
        out_shape=jax.ShapeDtypeStruct((n, HEAD), F32),
        compiler_params=_params(("parallel",)),
        name="moe_router",
    )(x, gain, router)


def _dispatch_plan(route, n_experts):
    n = route.shape[0]
    n_pick = 2 * n
    n_tiles = n_pick // MOE_TILE + n_experts
    e_flat = route[:, ROUTE_I1:ROUTE_I2 + 1].astype(jnp.int32).reshape(n_pick)
    onehot = (e_flat[:, None] == jnp.arange(n_experts, dtype=jnp.int32)[None, :]).astype(jnp.int32)
    csum = jnp.cumsum(onehot, axis=0)
    counts = csum[-1]
    rank = jnp.sum(onehot * csum, axis=1) - 1
    tiles_per = (counts + MOE_TILE - 1) // MOE_TILE
    tile_end = jnp.cumsum(tiles_per)
    tile_start = tile_end - tiles_per
    dest = jnp.sum(onehot * tile_start[None, :], axis=1) * MOE_TILE + rank
    pick = jnp.zeros((n_tiles * MOE_TILE,), jnp.int32).at[dest].set(jnp.arange(n_pick, dtype=jnp.int32))
    t = jnp.arange(n_tiles, dtype=jnp.int32)
    live = t < tile_end[-1]
    t_eff = jnp.minimum(t, tile_end[-1] - 1)
    tile_expert = jnp.sum((t_eff[:, None] >= tile_end[None, :]).astype(jnp.int32), axis=1)
    used = (t_eff - tile_start[tile_expert]) * MOE_TILE
    tile_rows = jnp.where(live, jnp.clip(counts[tile_expert] - used, 0, MOE_TILE), 0)
    return tile_expert, tile_rows.astype(jnp.int32), pick


def _moe_kernel(te_ref, rows_ref, pick_ref, x_hbm, g_ref, w1_ref, w3_ref, w2_ref, y_hbm,
                xg_ref, h_ref, acc_ref, yo_ref, sem_in, sem_out, *, n_tokens, per_step):
    i = pl.program_id(0)
    f = pl.program_id(1)
    n_tiles = pl.num_programs(0)
    n_ff = pl.num_programs(1)
    rows = rows_ref[i]
    slot = i % 2
    nxt = jnp.minimum(i + 1, n_tiles - 1)
    next_live = jnp.logical_and(i + 1 < n_tiles, rows_ref[nxt] > 0)

    def row_in(tile, p, to):
        tok = pick_ref[tile * MOE_TILE + p] >> 1
        return pltpu.make_async_copy(x_hbm.at[pl.ds(tok, 1)], xg_ref.at[to, pl.ds(p, 1)], sem_in.at[to])

    def row_out(p):
        pick = pick_ref[i * MOE_TILE + p]
        dst = (pick & 1) * n_tokens + (pick >> 1)
        return pltpu.make_async_copy(yo_ref.at[pl.ds(p, 1)], y_hbm.at[pl.ds(dst, 1)], sem_out)

    def rows_out_wait(n):
        whole = pl.multiple_of(n // 8 * 8, 8)

        @pl.when(whole > 0)
        def _():
            pltpu.make_async_copy(yo_ref.at[pl.ds(0, whole)], y_hbm.at[pl.ds(0, whole)], sem_out).wait()

        each(whole, n, lambda p: row_out(p).wait())

    def each(lo, hi, fn):
        def body(p, c):
            fn(p)
            return c

        lax.fori_loop(lo, hi, body, 0)

    def each_by_8(lo, n, fn):
        def body(c, carry):
            for u in range(8):
                fn(lo + c * 8 + u)
            return carry

        lax.fori_loop(0, n // 8, body, 0)
        each(lo + n // 8 * 8, lo + n, fn)

    @pl.when(jnp.logical_and(i == 0, f == 0))
    def _():
        each_by_8(0, MOE_TILE, lambda p: row_in(0, p, 0).start())

    @pl.when(jnp.logical_and(next_live, f * per_step < MOE_TILE))
    def _():
        each_by_8(f * per_step, per_step, lambda p: row_in(nxt, p, 1 - slot).start())

    @pl.when(rows > 0)
    def _():
        @pl.when(f == 0)
        def _():
            pltpu.make_async_copy(x_hbm.at[pl.ds(0, MOE_TILE)], xg_ref.at[slot], sem_in.at[slot]).wait()
            h_ref[...] = _rms_rows(xg_ref[slot], g_ref[...]).astype(BF16)
            acc_ref[...] = jnp.zeros_like(acc_ref)

        y = _swiglu_step(h_ref[...], w1_ref, w3_ref, w2_ref)

        @pl.when(f < n_ff - 1)
        def _():
            acc_ref[...] += y

        @pl.when(f == n_ff - 1)
        def _():
            @pl.when(i > 0)
            def _():
                rows_out_wait(rows_ref[jnp.maximum(i - 1, 0)])

            yo_ref[...] = acc_ref[...] + y
            each_by_8(0, rows, lambda p: row_out(p).start())

            @pl.when(jnp.logical_not(next_live))
            def _():
                rows_out_wait(rows)


def _moe_experts(x, gain, plan, w1, w3, w2, layer, idx):
    n, d = x.shape
    ff = w1.shape[-1]
    tile_expert, tile_rows, pick = plan
    n_tiles = tile_expert.shape[0]
    n_ff = ff // FF_TILE
    fblk = lambda i, f, rows: jnp.where(rows[i] > 0, f, n_ff - 1)
    per_step = min(c for c in range(8, MOE_TILE + 1, 8) if MOE_TILE % c == 0 and c * n_ff >= MOE_TILE)
    grid_spec = pltpu.PrefetchScalarGridSpec(
        num_scalar_prefetch=3,
        grid=(n_tiles, n_ff),
        in_specs=[pl.BlockSpec(memory_space=pl.ANY),
                  pl.BlockSpec((None, 1, d), lambda i, f, te, rows, pk: (layer, 0, 0)),
                  pl.BlockSpec((None, None, d, FF_TILE), lambda i, f, te, rows, pk: (idx, te[i], 0, fblk(i, f, rows))),
                  pl.BlockSpec((None, None, d, FF_TILE), lambda i, f, te, rows, pk: (idx, te[i], 0, fblk(i, f, rows))),
                  pl.BlockSpec((None, None, FF_TILE, d), lambda i, f, te, rows, pk: (idx, te[i], fblk(i, f, rows), 0))],
        out_specs=pl.BlockSpec(memory_space=pl.ANY),
        scratch_shapes=[pltpu.VMEM((2, MOE_TILE, d), F32), pltpu.VMEM((MOE_TILE, d), BF16),
                        pltpu.VMEM((MOE_TILE, d), F32), pltpu.VMEM((MOE_TILE, d), F32),
                        pltpu.SemaphoreType.DMA((2,)), pltpu.SemaphoreType.DMA],
    )
    return pl.pallas_call(
        functools.partial(_moe_kernel, n_tokens=n, per_step=per_step),
        grid_spec=grid_spec,
        out_shape=jax.ShapeDtypeStruct((2 * n, d), F32),
        compiler_params=pltpu.CompilerParams(dimension_semantics=("arbitrary", "arbitrary"),
                                             vmem_limit_bytes=VMEM_LIMIT, disable_bounds_checks=True),
        name="moe_experts",
    )(tile_expert, tile_rows, pick, x, gain, w1, w3, w2)


def _moe_combine_kernel(x_ref, y1_ref, y2_ref, r_ref, o_ref):
    r = r_ref[...]
    o_ref[...] = (x_ref[...] + r[:, ROUTE_G1:ROUTE_G1 + 1] * y1_ref[...]
                  + r[:, ROUTE_G2:ROUTE_G2 + 1] * y2_ref[...])


def _moe_combine(x, y, route):
    n, d = x.shape
    nt = n // TOKEN_TILE
    return pl.pallas_call(
        _moe_combine_kernel,
        grid=(nt,),
        in_specs=[pl.BlockSpec((TOKEN_TILE, d), lambda i: (i, 0)),
                  pl.BlockSpec((TOKEN_TILE, d), lambda i: (i, 0)),
                  pl.BlockSpec((TOKEN_TILE, d), lambda i: (nt + i, 0)),
                  pl.BlockSpec((TOKEN_TILE, HEAD), lambda i: (i, 0))],
        out_specs=pl.BlockSpec((TOKEN_TILE, d), lambda i: (i, 0)),
        out_shape=jax.ShapeDtypeStruct((n, d), F32),
        input_output_aliases={0: 0},
        compiler_params=_params(("parallel",)),
        name="moe_combine",
    )(x, y, y, route)


def _moe(x, gain, router, w1, w3, w2, layer, idx, n_experts):
    route = _router(x, gain, router, layer, idx, n_experts)
    y = _moe_experts(x, gain, _dispatch_plan(route, n_experts), w1, w3, w2, layer, idx)
    return _moe_combine(x, y, route)


def _final_norm_kernel(x_ref, g_ref, o_ref):
    o_ref[...] = _rms_rows(x_ref[...], g_ref[...])


def _final_norm(x, gain):
    n, d = x.shape
    return pl.pallas_call(
        _final_norm_kernel,
        grid=(n // TOKEN_TILE,),
        in_specs=[pl.BlockSpec((TOKEN_TILE, d), lambda i: (i, 0)),
                  pl.BlockSpec((1, d), lambda i: (0, 0))],
        out_specs=pl.BlockSpec((TOKEN_TILE, d), lambda i: (i, 0)),
        out_shape=jax.ShapeDtypeStruct((n, d), F32),
        compiler_params=_params(("parallel",)),
        name="final_norm",
    )(x, gain)


def _hgrn_rows(t_pad):
    n = t_pad // CHUNK
    best = max(c for c in range(1, n + 1) if n % c == 0 and c * CHUNK <= 2048)
    return best * CHUNK


def kernel(x_prompt, x_sample, cache_k, cache_v, state_hgrn, meta_tokens, mix_norm, w_in, lb_logits, hgrn_norm, sb_norm, w_out, ffn_norm, dense_w1, dense_w3, dense_w2, moe_router, moe_w1, moe_w3, moe_w2, final_norm):
    n_seq, seq, d = x_prompt.shape
    dec_b, dec_s, _ = x_sample.shape
    depth = w_in.shape[0]
    n_meta = meta_tokens.shape[0]
    heads = cache_k.shape[3]
    width = heads * HEAD
    t_real = n_meta + seq
    n_sample = dec_b * dec_s
    t_pad = next(t for t in range(_round_up(t_real, CHUNK), t_real + TOKEN_TILE * CHUNK, CHUNK)
                 if (n_seq * t + n_sample) % TOKEN_TILE == 0)
    n_prompt = n_seq * t_pad
    assert w_in.shape[2] == N_SEG * width and n_prompt % dec_s == 0 and dec_s % 8 == 0

    meta = jnp.broadcast_to(meta_tokens.astype(F32)[None], (n_seq, n_meta, d))
    slab = jnp.concatenate([meta, x_prompt, jnp.zeros((n_seq, t_pad - t_real, d), F32)], axis=1)
    x = jnp.concatenate([slab.reshape(n_prompt, d), x_sample.reshape(n_sample, d)], axis=0)

    p_lb = jax.nn.softmax(lb_logits.astype(F32), axis=0)
    lb_all = (jnp.cumsum(p_lb, axis=0) - p_lb[0]).reshape(depth, 1, width)

    w_in_bf = w_in.astype(BF16)
    w_out_bf = w_out.astype(BF16).reshape(depth, 2, width, d)
    dense_bf = [w.astype(BF16) for w in (dense_w1, dense_w3, dense_w2)]
    moe_bf = [w.astype(BF16) for w in (moe_w1, moe_w3, moe_w2)]
    n_experts = moe_router.shape[-1]
    router = jnp.pad(moe_router.astype(F32), ((0, 0), (0, 0), (0, HEAD - n_experts)))
    mix_g = mix_norm.reshape(depth, 1, d)
    ffn_g = ffn_norm.reshape(depth, 1, d)
    hgrn_g = hgrn_norm.reshape(depth, 1, HEAD)
    sb_g = sb_norm.reshape(depth, 1, HEAD)
    hgrn_m = jnp.asarray(_hgrn_masks(), BF16)
    sb_u = jnp.asarray(_sb_masks(), BF16)
    rows_p = _hgrn_rows(t_pad)

    kp, vp, sp, ks, vs, ss = [], [], [], [], [], []
    for l in range(depth):
        p7 = _inproj(x, mix_g, w_in_bf, l, width)
        ya_p, s_p = _hgrn(p7, lb_all, hgrn_g, hgrn_m, l, n_seq=n_seq, rows=rows_p, n_blk=t_pad // rows_p,
                          row_blk0=0, t_valid=t_real)
        ya_s, s_s = _hgrn(p7, lb_all, hgrn_g, hgrn_m, l, n_seq=dec_b, rows=dec_s, n_blk=1,
                          row_blk0=n_prompt // dec_s, t_valid=dec_s, s0=state_hgrn)
        yb_p = _sb_prompt(p7, sb_g, sb_u, l, n_seq=n_seq, t_pad=t_pad)
        yb_s = _sb_sample(p7, cache_k, cache_v, sb_g, sb_u, l, n_seq=dec_b, rows=dec_s,
                          row_blk0=n_prompt // dec_s)
        x = _outproj((ya_p, yb_p), (ya_s, yb_s), w_out_bf, x, l)
        if l % 2 == 0:
            x = _ffn(x, ffn_g, *dense_bf, l, l // 2)
        else:
            x = _moe(x, ffn_g, router, *moe_bf, l, l // 2, n_experts)

        kv = lambda seg: p7[seg, :n_prompt].reshape(n_seq, t_pad, heads, HEAD)[:, :t_real]
        kv_s = lambda seg: p7[seg, n_prompt:n_prompt + n_sample].reshape(dec_b, dec_s, heads, HEAD)
        kp.append(kv(5))
        vp.append(kv(6))
        sp.append(s_p)
        ks.append(kv_s(5))
        vs.append(kv_s(6))
        ss.append(s_s)

    y = _final_norm(x, final_norm.reshape(1, d))
    y_prompt = y[:n_prompt].reshape(n_seq, t_pad, d)[:, n_meta:t_real]
    y_sample = y[n_prompt:n_prompt + n_sample].reshape(dec_b, dec_s, d)
    return (y_prompt, y_sample, jnp.stack(kp), jnp.stack(vp), jnp.stack(sp),
            jnp.stack(ks), jnp.stack(vs), jnp.stack(ss))
```

```python
import functools

import numpy as np
import jax
import jax.numpy as jnp
from jax import lax
from jax.experimental import pallas as pl
from jax.experimental.pallas import tpu as pltpu

F32 = jnp.float32
BF16 = jnp.bfloat16
EPS = 1e-6
LB_FLOOR = 1e-20
HEAD = 128
CHUNK = 128
LEVELS = 7
N_SEG = 7
TOKEN_TILE = 512
COL_TILE = 2048
IN_ROW_TILE = 1024
IN_COL_TILE = 1024
MOE_TILE = 512
FF_TILE = 512
VMEM_LIMIT = 56 * 1024 * 1024
HGRN_GROUP = 4
SB_GROUP = 6
SAMPLE_KEYS = 512
SB_EXIT = -104.0
ROUTE_I1, ROUTE_I2, ROUTE_G1, ROUTE_G2 = 8, 9, 10, 11


def _round_up(x, m):
    return (x + m - 1) // m * m


def _params(sem):
    return pltpu.CompilerParams(dimension_semantics=sem, vmem_limit_bytes=VMEM_LIMIT)


def _dot(a, b):
    return jnp.dot(a, b, preferred_element_type=F32)


def _dot_nt(a, b):
    return lax.dot_general(a, b, (((1,), (1,)), ((), ())), preferred_element_type=F32)


def _split2(x):
    hi = x.astype(BF16)
    lo = (x - hi.astype(F32)).astype(BF16)
    return hi, lo


def _rms_rows(x, g):
    return x * lax.rsqrt(jnp.mean(x * x, axis=-1, keepdims=True) + EPS) * g


def _inproj_kernel(x_ref, g_ref, w_ref, o_ref, h_ref):
    @pl.when(pl.program_id(1) == 0)
    def _():
        h_ref[...] = _rms_rows(x_ref[...], g_ref[...]).astype(BF16)

    o_ref[...] = _dot(h_ref[...], w_ref[...])


def _inproj(x, gain, w_bf, layer, width):
    n, d = x.shape
    per_seg = width // IN_COL_TILE
    rows = IN_ROW_TILE if n % IN_ROW_TILE == 0 else TOKEN_TILE
    grid = (n // rows, N_SEG * per_seg)
    return pl.pallas_call(
        _inproj_kernel,
        grid=grid,
        in_specs=[
            pl.BlockSpec((rows, d), lambda i, j: (i, 0)),
            pl.BlockSpec((None, 1, d), lambda i, j: (layer, 0, 0)),
            pl.BlockSpec((None, d, IN_COL_TILE), lambda i, j: (layer, 0, j)),
        ],
        out_specs=pl.BlockSpec((None, rows, IN_COL_TILE), lambda i, j: (j // per_seg, i, j % per_seg)),
        out_shape=jax.ShapeDtypeStruct((N_SEG, n, width), F32),
        scratch_shapes=[pltpu.VMEM((rows, d), BF16)],
        compiler_params=_params(("parallel", "arbitrary")),
        name="inproj",
    )(x, gain, w_bf)


def _hgrn_masks():
    c = CHUNK
    t = np.arange(c)[:, None]
    j = np.arange(c)[None, :]
    blocks = [(j <= t)]
    for l in range(LEVELS):
        hs = 1 << l
        mid = (t >> (l + 1) << (l + 1)) + hs
        upper = ((t >> l) & 1) == 1
        blocks.append(np.where(upper, (j >= mid) & (j <= t), (j > t) & (j <= mid - 1)))
    blocks.append(j > t)
    return np.concatenate(blocks, axis=0).astype(np.float32)


def _hgrn_chunk(q, fa, v, ga, valid, lbc, gain, m_ref, st_ref, group):
    log_lb, log1m_lb, one_m_lb = lbc
    e = jnp.exp(-jnp.abs(fa))
    log_sig = jnp.minimum(fa, 0.0) - jnp.log1p(e)
    bterm = log1m_lb + log_sig
    lf = jnp.maximum(log_lb, bterm) + jnp.log1p(jnp.exp(-jnp.abs(log_lb - bterm)))
    k = one_m_lb * (jnp.where(fa > 0.0, e, 1.0) / (1.0 + e))
    lf = jnp.where(valid, lf, 0.0)
    k = jnp.where(valid, k, 0.0)

    lf_hi, lf_lo = _split2(lf)
    m = m_ref[...]
    dall = _dot(m, lf_hi) + _dot(m, lf_lo)
    b = dall[0:CHUNK]
    d_last = dall[(LEVELS + 1) * CHUNK:(LEVELS + 2) * CHUNK]

    heads = [slice(hh * HEAD, (hh + 1) * HEAD) for hh in range(group)]
    row = lax.broadcasted_iota(jnp.int32, q.shape, 0)
    ti = lax.broadcasted_iota(jnp.int32, (CHUNK, CHUNK), 0)
    si = lax.broadcasted_iota(jnp.int32, (CHUNK, CHUNK), 1)
    top_bit = 31 - lax.clz(ti ^ si)
    owner = jnp.where(si < ti, top_bit, jnp.where(si == ti, -1, -2))

    q_bf = q.astype(BF16)
    k_bf = k.astype(BF16)
    a = [jnp.where(owner == -1, _dot_nt(q_bf[:, hd], k_bf[:, hd]), 0.0) for hd in heads]
    for l in range(LEVELS):
        upper = ((row >> l) & 1) == 1
        z = (jnp.where(upper, q, k) * jnp.exp(dall[(l + 1) * CHUNK:(l + 2) * CHUNK])).astype(BF16)
        a = [jnp.where(owner == l, _dot_nt(z[:, hd], z[:, hd]), a[hh]) for hh, hd in enumerate(heads)]

    v_bf = v.astype(BF16)
    qe = (q * jnp.exp(b)).astype(BF16)
    w = (k * jnp.exp(d_last)).astype(BF16)
    keep = jnp.exp(b[CHUNK - 1:CHUNK, :])
    st = [st_ref[hh] for hh in range(group)]
    o = [_dot(a[hh].astype(BF16), v_bf[:, hd]) + _dot_nt(qe[:, hd], st[hh].astype(BF16))
         for hh, hd in enumerate(heads)]
    for hh, hd in enumerate(heads):
        st_ref[hh] = st[hh] * keep[:, hd] + _dot(v[:, hd].T.astype(BF16), w[:, hd])

    y = jnp.concatenate([_rms_rows(oh, gain) for oh in o], axis=1)
    return (y * (ga / (1.0 + jnp.exp(-ga)))).astype(BF16)


def _hgrn_kernel(*refs, rows, t_valid, group, has_s0):
    q_ref, f_ref, i_ref, g_ref, lb_ref, gain_ref, m_ref = refs[:7]
    s0_ref = refs[7] if has_s0 else None
    y_ref, s_ref, st_ref = refs[-3:]
    r = pl.program_id(2)

    @pl.when(r == 0)
    def _():
        for hh in range(group):
            st_ref[hh] = s0_ref[hh].T if has_s0 else jnp.zeros((HEAD, HEAD), F32)

    gain = gain_ref[...]
    lb = lb_ref[...]
    lbc = (jnp.log(jnp.maximum(lb, LB_FLOOR)), jnp.log1p(-lb), 1.0 - lb)
    row = lax.broadcasted_iota(jnp.int32, (CHUNK, group * HEAD), 0)

    if rows < CHUNK:
        pad = jnp.zeros((CHUNK - rows, group * HEAD), F32)
        ld = lambda ref: jnp.concatenate([ref[...], pad], axis=0)
        y = _hgrn_chunk(ld(q_ref), ld(f_ref), ld(i_ref), ld(g_ref), row < t_valid, lbc, gain, m_ref, st_ref, group)
        y_ref[...] = y[:rows]
    else:
        def body(c, carry):
            sl = pl.ds(pl.multiple_of(c * CHUNK, CHUNK), CHUNK)
            valid = (r * rows + c * CHUNK + row) < t_valid
            y_ref[sl, :] = _hgrn_chunk(q_ref[sl, :], f_ref[sl, :], i_ref[sl, :], g_ref[sl, :],
                                       valid, lbc, gain, m_ref, st_ref, group)
            return carry

        lax.fori_loop(0, rows // CHUNK, body, 0)

    @pl.when(r == pl.num_programs(2) - 1)
    def _():
        for hh in range(group):
            s_ref[hh] = st_ref[hh].T


def _hgrn(p7, lb, gain, masks, layer, *, n_seq, rows, n_blk, row_blk0, t_valid, s0=None):
    _, n, width = p7.shape
    heads = width // HEAD
    group = HGRN_GROUP
    gw = group * HEAD
    seg_spec = lambda seg: pl.BlockSpec((None, rows, gw),
                                        lambda b, h, r: (seg, row_blk0 + b * n_blk + r, h))
    in_specs = [seg_spec(0), seg_spec(1), seg_spec(2), seg_spec(3),
                pl.BlockSpec((None, 1, gw), lambda b, h, r: (layer, 0, h)),
                pl.BlockSpec((None, 1, HEAD), lambda b, h, r: (layer, 0, 0)),
                pl.BlockSpec(masks.shape, lambda b, h, r: (0, 0))]
    args = [p7, p7, p7, p7, lb, gain, masks]
    if s0 is not None:
        in_specs.append(pl.BlockSpec((None, None, group, HEAD, HEAD), lambda b, h, r: (layer, b, h, 0, 0)))
        args.append(s0)
    kern = functools.partial(_hgrn_kernel, rows=rows, t_valid=t_valid, group=group, has_s0=s0 is not None)
    return pl.pallas_call(
        kern,
        grid=(n_seq, heads // group, n_blk),
        in_specs=in_specs,
        out_specs=[pl.BlockSpec((rows, gw), lambda b, h, r: (b * n_blk + r, h)),
                   pl.BlockSpec((None, group, HEAD, HEAD), lambda b, h, r: (b, h, 0, 0))],
        out_shape=[jax.ShapeDtypeStruct((n_seq * n_blk * rows, width), BF16),
                   jax.ShapeDtypeStruct((n_seq, heads, HEAD, HEAD), F32)],
        scratch_shapes=[pltpu.VMEM((group, HEAD, HEAD), F32)],
        compiler_params=_params(("parallel", "parallel", "arbitrary")),
        name="hgrn_sample" if s0 is not None else "hgrn_prompt",
    )(*args)


def _sb_masks():
    j = np.arange(CHUNK)[:, None]
    s = np.arange(CHUNK)[None, :]
    return np.concatenate([np.ones((CHUNK, CHUNK)), (j > s)], axis=1).astype(np.float32)


def _sb_tiles(q_bf, k_t, v_t, mask, u_ref, r_ref, acc_ref, scale):
    g, tq, _ = q_bf.shape
    z = lax.dot_general(q_bf, k_t.astype(BF16), (((2,), (2,)), ((0,), (0,))), preferred_element_type=F32) * scale
    log_beta = jnp.minimum(z, 0.0) - jnp.log1p(jnp.exp(-jnp.abs(z)))
    log_rest = log_beta - z
    if mask is not None:
        log_rest = jnp.where(mask, log_rest, 0.0)
    hi, lo = _split2(log_rest.reshape(g * tq, CHUNK))
    u = u_ref[...]
    c = (_dot(hi, u) + _dot(lo, u)).reshape(g, tq, 2 * CHUNK)
    r = r_ref[...]
    w = jnp.exp(log_beta + r + c[:, :, CHUNK:])
    if mask is not None:
        w = jnp.where(mask, w, 0.0)
    acc_ref[...] += lax.dot_general(w.astype(BF16), v_t.astype(BF16), (((2,), (1,)), ((0,), (0,))),
                                    preferred_element_type=F32)
    r_ref[...] = r + c[:, :, :CHUNK]


def _sb_prompt_kernel(q_ref, k_ref, v_ref, gain_ref, u_ref, y_ref, r_ref, acc_ref, *, scale, sub):
    first = pl.program_id(2) * sub
    r_ref[...] = jnp.zeros_like(r_ref)
    acc_ref[...] = jnp.zeros_like(acc_ref)
    ti = lax.broadcasted_iota(jnp.int32, (sub, CHUNK, CHUNK), 1)
    si = lax.broadcasted_iota(jnp.int32, (sub, CHUNK, CHUNK), 2)
    ai = lax.broadcasted_iota(jnp.int32, (sub, CHUNK, CHUNK), 0)
    q_bf = q_ref[...].astype(BF16).reshape(sub, CHUNK, HEAD)

    def tiles(step, mask):
        rows = [pl.ds(pl.multiple_of(jnp.maximum(first + a - step, 0) * CHUNK, CHUNK), CHUNK) for a in range(sub)]
        k_t = jnp.stack([k_ref[sl, :] for sl in rows])
        v_t = jnp.stack([v_ref[sl, :] for sl in rows])
        _sb_tiles(q_bf, k_t, v_t, mask, u_ref, r_ref, acc_ref, scale)

    tiles(0, si < ti)

    def cond(c):
        step, worst = c
        return jnp.logical_and(step < first + sub, worst > SB_EXIT)

    def body(c):
        step, _ = c
        tiles(step, first + ai - step >= 0)
        return step + 1, jnp.max(r_ref[...])

    lax.while_loop(cond, body, (jnp.int32(1), jnp.max(r_ref[...])))
    y = _rms_rows(acc_ref[...], gain_ref[...]).astype(BF16)
    y_ref[...] = y.reshape(sub * CHUNK, HEAD)


def _sb_prompt(p7, gain, u, layer, *, n_seq, t_pad):
    _, n, width = p7.shape
    heads = width // HEAD
    n_q = t_pad // CHUNK
    sub = max(c for c in range(1, SB_GROUP + 1) if n_q % c == 0)
    n_g = n_q // sub
    kern = functools.partial(_sb_prompt_kernel, scale=HEAD ** -0.5, sub=sub)
    return pl.pallas_call(
        kern,
        grid=(n_seq, heads, n_g),
        in_specs=[pl.BlockSpec((None, sub * CHUNK, HEAD), lambda b, h, i: (4, b * n_g + i, h)),
                  pl.BlockSpec((None, t_pad, HEAD), lambda b, h, i: (5, b, h)),
                  pl.BlockSpec((None, t_pad, HEAD), lambda b, h, i: (6, b, h)),
                  pl.BlockSpec((None, 1, HEAD), lambda b, h, i: (layer, 0, 0)),
                  pl.BlockSpec(u.shape, lambda b, h, i: (0, 0))],
        out_specs=pl.BlockSpec((sub * CHUNK, HEAD), lambda b, h, i: (b * n_g + i, h)),
        out_shape=jax.ShapeDtypeStruct((n_seq * t_pad, width), BF16),
        scratch_shapes=[pltpu.VMEM((sub, CHUNK, CHUNK), F32), pltpu.VMEM((sub, CHUNK, HEAD), F32)],
        compiler_params=_params(("parallel", "parallel", "arbitrary")),
        name="sb_prompt",
    )(p7, p7, p7, gain, u)


def _sb_sample_kernel(q_ref, kn_ref, vn_ref, kc_ref, vc_ref, gain_ref, u_ref, y_ref,
                      qb_ref, r_ref, acc_ref, *, scale, rows, heads, n_cache):
    kt = pl.program_id(1)
    n_kt = pl.num_programs(1)
    cols = [slice(h * HEAD, (h + 1) * HEAD) for h in range(heads)]
    ti = lax.broadcasted_iota(jnp.int32, (heads, rows, CHUNK), 1)
    si = lax.broadcasted_iota(jnp.int32, (heads, rows, CHUNK), 2)

    @pl.when(kt == 0)
    def _():
        qb_ref[...] = jnp.stack([q_ref[:, c] for c in cols]).astype(BF16)
        r_ref[...] = jnp.zeros_like(r_ref)
        acc_ref[...] = jnp.zeros_like(acc_ref)
        pad = jnp.zeros((CHUNK - rows, HEAD), F32)
        k_t = jnp.stack([jnp.concatenate([kn_ref[:, c], pad], axis=0) for c in cols])
        v_t = jnp.stack([jnp.concatenate([vn_ref[:, c], pad], axis=0) for c in cols])
        _sb_tiles(qb_ref[...], k_t, v_t, si < ti, u_ref, r_ref, acc_ref, scale)

    block_left = n_cache - (n_kt - 1 - kt) * SAMPLE_KEYS
    for part in reversed(range(SAMPLE_KEYS // CHUNK)):
        left = block_left - part * CHUNK
        rows_k = slice(part * CHUNK, (part + 1) * CHUNK)

        @pl.when(jnp.logical_and(left > 0, jnp.max(r_ref[...]) > SB_EXIT))
        def _():
            live = lax.broadcasted_iota(jnp.int32, (heads, CHUNK, HEAD), 1) < left
            k_t = jnp.where(live, jnp.stack([kc_ref[rows_k, h, :] for h in range(heads)]), 0.0)
            v_t = jnp.where(live, jnp.stack([vc_ref[rows_k, h, :] for h in range(heads)]), 0.0)
            _sb_tiles(qb_ref[...], k_t, v_t, si < left, u_ref, r_ref, acc_ref, scale)

    @pl.when(kt == n_kt - 1)
    def _():
        gain = gain_ref[...]
        y_ref[...] = jnp.concatenate([_rms_rows(acc_ref[h], gain) for h in range(heads)], axis=1).astype(BF16)


def _sb_sample(p7, cache_k, cache_v, gain, u, layer, *, n_seq, rows, row_blk0):
    _, n, width = p7.shape
    n_cache, heads = cache_k.shape[2:4]
    assert rows <= CHUNK and width == heads * HEAD
    n_kt = pl.cdiv(n_cache, SAMPLE_KEYS)
    kern = functools.partial(_sb_sample_kernel, scale=HEAD ** -0.5, rows=rows, heads=heads, n_cache=n_cache)
    new_spec = lambda seg: pl.BlockSpec((None, rows, width), lambda b, t: (seg, row_blk0 + b, 0))
    cache_spec = pl.BlockSpec((None, None, SAMPLE_KEYS, heads, HEAD), lambda b, t: (layer, b, n_kt - 1 - t, 0, 0))
    return pl.pallas_call(
        kern,
        grid=(n_seq, n_kt),
        in_specs=[new_spec(4), new_spec(5), new_spec(6), cache_spec, cache_spec,
                  pl.BlockSpec((None, 1, HEAD), lambda b, t: (layer, 0, 0)),
                  pl.BlockSpec(u.shape, lambda b, t: (0, 0))],
        out_specs=pl.BlockSpec((rows, width), lambda b, t: (b, 0)),
        out_shape=jax.ShapeDtypeStruct((n_seq * rows, width), BF16),
        scratch_shapes=[pltpu.VMEM((heads, rows, HEAD), BF16), pltpu.VMEM((heads, rows, CHUNK), F32),
                        pltpu.VMEM((heads, rows, HEAD), F32)],
        compiler_params=_params(("parallel", "arbitrary")),
        name="sb_sample",
    )(p7, p7, p7, cache_k, cache_v, gain, u)


def _outproj_kernel(yap_ref, ybp_ref, yas_ref, ybs_ref, wa_ref, wb_ref, x_ref, o_ref, *, prompt_tiles):
    def mix(ya_ref, yb_ref):
        o_ref[...] = x_ref[...] + _dot(ya_ref[...], wa_ref[...]) + _dot(yb_ref[...], wb_ref[...])

    is_prompt = pl.program_id(0) < prompt_tiles
    pl.when(is_prompt)(lambda: mix(yap_ref, ybp_ref))
    pl.when(jnp.logical_not(is_prompt))(lambda: mix(yas_ref, ybs_ref))


def _outproj(y_prompt, y_sample, w_bf, x, layer):
    n, d = x.shape
    width = y_prompt[0].shape[1]
    pt, st = y_prompt[0].shape[0] // TOKEN_TILE, y_sample[0].shape[0] // TOKEN_TILE
    assert pt * TOKEN_TILE == y_prompt[0].shape[0] and st * TOKEN_TILE == y_sample[0].shape[0] and pt + st == n // TOKEN_TILE
    p_spec = pl.BlockSpec((TOKEN_TILE, width), lambda i, j: (jnp.minimum(i, pt - 1), 0))
    s_spec = pl.BlockSpec((TOKEN_TILE, width), lambda i, j: (jnp.maximum(i - pt, 0), 0))
    return pl.pallas_call(
        functools.partial(_outproj_kernel, prompt_tiles=pt),
        grid=(n // TOKEN_TILE, d // COL_TILE),
        in_specs=[p_spec, p_spec, s_spec, s_spec,
                  pl.BlockSpec((None, None, width, COL_TILE), lambda i, j: (layer, 0, 0, j)),
                  pl.BlockSpec((None, None, width, COL_TILE), lambda i, j: (layer, 1, 0, j)),
                  pl.BlockSpec((TOKEN_TILE, COL_TILE), lambda i, j: (i, j))],
        out_specs=pl.BlockSpec((TOKEN_TILE, COL_TILE), lambda i, j: (i, j)),
        out_shape=jax.ShapeDtypeStruct((n, d), F32),
        input_output_aliases={6: 0},
        compiler_params=_params(("parallel", "arbitrary")),
        name="outproj",
    )(*y_prompt, *y_sample, w_bf, w_bf, x)


def _swiglu_step(h, w1_ref, w3_ref, w2_ref):
    a = _dot(h, w1_ref[...])
    b = _dot(h, w3_ref[...])
    g = (a / (1.0 + jnp.exp(-a))) * b
    return _dot(g.astype(BF16), w2_ref[...])


def _ffn_kernel(x_ref, g_ref, w1_ref, w3_ref, w2_ref, o_ref, h_ref):
    @pl.when(pl.program_id(1) == 0)
    def _():
        x = x_ref[...]
        h_ref[...] = _rms_rows(x, g_ref[...]).astype(BF16)
        o_ref[...] = x

    o_ref[...] += _swiglu_step(h_ref[...], w1_ref, w3_ref, w2_ref)


def _ffn(x, gain, w1, w3, w2, layer, idx):
    n, d = x.shape
    ff = w1.shape[-1]
    return pl.pallas_call(
        _ffn_kernel,
        grid=(n // TOKEN_TILE, ff // FF_TILE),
        in_specs=[pl.BlockSpec((TOKEN_TILE, d), lambda i, f: (i, 0)),
                  pl.BlockSpec((None, 1, d), lambda i, f: (layer, 0, 0)),
                  pl.BlockSpec((None, d, FF_TILE), lambda i, f: (idx, 0, f)),
                  pl.BlockSpec((None, d, FF_TILE), lambda i, f: (idx, 0, f)),
                  pl.BlockSpec((None, FF_TILE, d), lambda i, f: (idx, f, 0))],
        out_specs=pl.BlockSpec((TOKEN_TILE, d), lambda i, f: (i, 0)),
        out_shape=jax.ShapeDtypeStruct((n, d), F32),
        scratch_shapes=[pltpu.VMEM((TOKEN_TILE, d), BF16)],
        compiler_params=_params(("parallel", "arbitrary")),
        name="ffn_dense",
    )(x, gain, w1, w3, w2)


def _route(h, rt_ref, n_experts):
    h_hi, h_lo = _split2(h)
    r = rt_ref[...]
    r_hi, r_lo = _split2(r)
    logits = _dot(h_hi, r_hi) + _dot(h_lo, r_hi) + _dot(h_hi, r_lo)
    lane = lax.broadcasted_iota(jnp.int32, logits.shape, 1)
    live = lane < n_experts
    logits = jnp.where(live, logits, -jnp.inf)
    ex = jnp.exp(logits - jnp.max(logits, axis=-1, keepdims=True))
    p = ex / jnp.sum(ex, axis=-1, keepdims=True)
    p = jnp.where(live, p, -1.0)
    p1 = jnp.max(p, axis=-1, keepdims=True)
    i1 = jnp.min(jnp.where(p == p1, lane, HEAD), axis=-1, keepdims=True)
    rest = jnp.where(lane == i1, -1.0, p)
    p2 = jnp.max(rest, axis=-1, keepdims=True)
    i2 = jnp.min(jnp.where(rest == p2, lane, HEAD), axis=-1, keepdims=True)
    tot = p1 + p2
    g1, g2 = p1 / tot, p2 / tot
    return jnp.where(lane == i1, g1, 0.0) + jnp.where(lane == i2, g2, 0.0), i1, i2, g1, g2


def _router_kernel(x_ref, g_ref, rt_ref, o_ref, *, n_experts):
    gates, i1, i2, g1, g2 = _route(_rms_rows(x_ref[...], g_ref[...]), rt_ref, n_experts)
    lane = lax.broadcasted_iota(jnp.int32, gates.shape, 1)
    rec = jnp.where(lane == ROUTE_I1, i1.astype(F32), gates)
    rec = jnp.where(lane == ROUTE_I2, i2.astype(F32), rec)
    rec = jnp.where(lane == ROUTE_G1, g1, rec)
    o_ref[...] = jnp.where(lane == ROUTE_G2, g2, rec)


def _router(x, gain, router, layer, idx, n_experts):
    n, d = x.shape
    assert n_experts <= ROUTE_I1
    return pl.pallas_call(
        functools.partial(_router_kernel, n_experts=n_experts),
        grid=(n // TOKEN_TILE,),
        in_specs=[pl.BlockSpec((TOKEN_TILE, d), lambda i: (i, 0)),
                  pl.BlockSpec((None, 1, d), lambda i: (layer, 0, 0)),
                  pl.BlockSpec((None, d, HEAD), lambda i: (idx, 0, 0))],
        out_specs=pl.BlockSpec((TOKEN_TILE, HEAD), lambda i: (i, 0)),
        out_shape=jax.ShapeDtypeStruct((n, HEAD), F32),
        compiler_params=_params(("parallel",)),
        name="moe_router",
    )(x, gain, router)


def _dispatch_plan(route, n_experts):
    n = route.shape[0]
    n_pick = 2 * n
    n_tiles = n_pick // MOE_TILE + n_experts
    e_flat = route[:, ROUTE_I1:ROUTE_I2 + 1].astype(jnp.int32).reshape(n_pick)
    onehot = (e_flat[:, None] == jnp.arange(n_experts, dtype=jnp.int32)[None, :]).astype(jnp.int32)
    csum = jnp.cumsum(onehot, axis=0)
    counts = csum[-1]
    rank = jnp.sum(onehot * csum, axis=1) - 1
    tiles_per = (counts + MOE_TILE - 1) // MOE_TILE
    tile_end = jnp.cumsum(tiles_per)
    tile_start = tile_end - tiles_per
    dest = jnp.sum(onehot * tile_start[None, :], axis=1) * MOE_TILE + rank
    pick = jnp.zeros((n_tiles * MOE_TILE,), jnp.int32).at[dest].set(jnp.arange(n_pick, dtype=jnp.int32))
    t = jnp.arange(n_tiles, dtype=jnp.int32)
    live = t < tile_end[-1]
    t_eff = jnp.minimum(t, tile_end[-1] - 1)
    tile_expert = jnp.sum((t_eff[:, None] >= tile_end[None, :]).astype(jnp.int32), axis=1)
    used = (t_eff - tile_start[tile_expert]) * MOE_TILE
    tile_rows = jnp.where(live, jnp.clip(counts[tile_expert] - used, 0, MOE_TILE), 0)
    return tile_expert, tile_rows.astype(jnp.int32), pick


def _moe_kernel(te_ref, rows_ref, pick_ref, x_hbm, g_ref, w1_ref, w3_ref, w2_ref, y_hbm,
                xg_ref, h_ref, acc_ref, yo_ref, sem_in, sem_out, *, n_tokens, per_step):
    i = pl.program_id(0)
    f = pl.program_id(1)
    n_tiles = pl.num_programs(0)
    n_ff = pl.num_programs(1)
    rows = rows_ref[i]
    slot = i % 2
    nxt = jnp.minimum(i + 1, n_tiles - 1)
    next_live = jnp.logical_and(i + 1 < n_tiles, rows_ref[nxt] > 0)

    def row_in(tile, p, to):
        tok = pick_ref[tile * MOE_TILE + p] >> 1
        return pltpu.make_async_copy(x_hbm.at[pl.ds(tok, 1)], xg_ref.at[to, pl.ds(p, 1)], sem_in.at[to])

    def row_out(p):
        pick = pick_ref[i * MOE_TILE + p]
        dst = (pick & 1) * n_tokens + (pick >> 1)
        return pltpu.make_async_copy(yo_ref.at[pl.ds(p, 1)], y_hbm.at[pl.ds(dst, 1)], sem_out)

    def rows_out_wait(n):
        whole = pl.multiple_of(n // 8 * 8, 8)

        @pl.when(whole > 0)
        def _():
            pltpu.make_async_copy(yo_ref.at[pl.ds(0, whole)], y_hbm.at[pl.ds(0, whole)], sem_out).wait()

        each(whole, n, lambda p: row_out(p).wait())

    def each(lo, hi, fn):
        def body(p, c):
            fn(p)
            return c

        lax.fori_loop(lo, hi, body, 0)

    def each_by_8(lo, n, fn):
        def body(c, carry):
            for u in range(8):
                fn(lo + c * 8 + u)
            return carry

        lax.fori_loop(0, n // 8, body, 0)
        each(lo + n // 8 * 8, lo + n, fn)

    @pl.when(jnp.logical_and(i == 0, f == 0))
    def _():
        each_by_8(0, MOE_TILE, lambda p: row_in(0, p, 0).start())

    @pl.when(jnp.logical_and(next_live, f * per_step < MOE_TILE))
    def _():
        each_by_8(f * per_step, per_step, lambda p: row_in(nxt, p, 1 - slot).start())

    @pl.when(rows > 0)
    def _():
        @pl.when(f == 0)
        def _():
            pltpu.make_async_copy(x_hbm.at[pl.ds(0, MOE_TILE)], xg_ref.at[slot], sem_in.at[slot]).wait()
            h_ref[...] = _rms_rows(xg_ref[slot], g_ref[...]).astype(BF16)
            acc_ref[...] = jnp.zeros_like(acc_ref)

        y = _swiglu_step(h_ref[...], w1_ref, w3_ref, w2_ref)

        @pl.when(f < n_ff - 1)
        def _():
            acc_ref[...] += y

        @pl.when(f == n_ff - 1)
        def _():
            @pl.when(i > 0)
            def _():
                rows_out_wait(rows_ref[jnp.maximum(i - 1, 0)])

            yo_ref[...] = acc_ref[...] + y
            each_by_8(0, rows, lambda p: row_out(p).start())

            @pl.when(jnp.logical_not(next_live))
            def _():
                rows_out_wait(rows)


def _moe_experts(x, gain, plan, w1, w3, w2, layer, idx):
    n, d = x.shape
    ff = w1.shape[-1]
    tile_expert, tile_rows, pick = plan
    n_tiles = tile_expert.shape[0]
    n_ff = ff // FF_TILE
    fblk = lambda i, f, rows: jnp.where(rows[i] > 0, f, n_ff - 1)
    per_step = min(c for c in range(8, MOE_TILE + 1, 8) if MOE_TILE % c == 0 and c * n_ff >= MOE_TILE)
    grid_spec = pltpu.PrefetchScalarGridSpec(
        num_scalar_prefetch=3,
        grid=(n_tiles, n_ff),
        in_specs=[pl.BlockSpec(memory_space=pl.ANY),
                  pl.BlockSpec((None, 1, d), lambda i, f, te, rows, pk: (layer, 0, 0)),
                  pl.BlockSpec((None, None, d, FF_TILE), lambda i, f, te, rows, pk: (idx, te[i], 0, fblk(i, f, rows))),
                  pl.BlockSpec((None, None, d, FF_TILE), lambda i, f, te, rows, pk: (idx, te[i], 0, fblk(i, f, rows))),
                  pl.BlockSpec((None, None, FF_TILE, d), lambda i, f, te, rows, pk: (idx, te[i], fblk(i, f, rows), 0))],
        out_specs=pl.BlockSpec(memory_space=pl.ANY),
        scratch_shapes=[pltpu.VMEM((2, MOE_TILE, d), F32), pltpu.VMEM((MOE_TILE, d), BF16),
                        pltpu.VMEM((MOE_TILE, d), F32), pltpu.VMEM((MOE_TILE, d), F32),
                        pltpu.SemaphoreType.DMA((2,)), pltpu.SemaphoreType.DMA],
    )
    return pl.pallas_call(
        functools.partial(_moe_kernel, n_tokens=n, per_step=per_step),
        grid_spec=grid_spec,
        out_shape=jax.ShapeDtypeStruct((2 * n, d), F32),
        compiler_params=pltpu.CompilerParams(dimension_semantics=("arbitrary", "arbitrary"),
                                             vmem_limit_bytes=VMEM_LIMIT, disable_bounds_checks=True),
        name="moe_experts",
    )(tile_expert, tile_rows, pick, x, gain, w1, w3, w2)


def _moe_combine_kernel(x_ref, y1_ref, y2_ref, r_ref, o_ref):
    r = r_ref[...]
    o_ref[...] = (x_ref[...] + r[:, ROUTE_G1:ROUTE_G1 + 1] * y1_ref[...]
                  + r[:, ROUTE_G2:ROUTE_G2 + 1] * y2_ref[...])


def _moe_combine(x, y, route):
    n, d = x.shape
    nt = n // TOKEN_TILE
    return pl.pallas_call(
        _moe_combine_kernel,
        grid=(nt,),
        in_specs=[pl.BlockSpec((TOKEN_TILE, d), lambda i: (i, 0)),
                  pl.BlockSpec((TOKEN_TILE, d), lambda i: (i, 0)),
                  pl.BlockSpec((TOKEN_TILE, d), lambda i: (nt + i, 0)),
                  pl.BlockSpec((TOKEN_TILE, HEAD), lambda i: (i, 0))],
        out_specs=pl.BlockSpec((TOKEN_TILE, d), lambda i: (i, 0)),
        out_shape=jax.ShapeDtypeStruct((n, d), F32),
        input_output_aliases={0: 0},
        compiler_params=_params(("parallel",)),
        name="moe_combine",
    )(x, y, y, route)


def _moe(x, gain, router, w1, w3, w2, layer, idx, n_experts):
    route = _router(x, gain, router, layer, idx, n_experts)
    y = _moe_experts(x, gain, _dispatch_plan(route, n_experts), w1, w3, w2, layer, idx)
    return _moe_combine(x, y, route)


def _final_norm_kernel(x_ref, g_ref, o_ref):
    o_ref[...] = _rms_rows(x_ref[...], g_ref[...])


def _final_norm(x, gain):
    n, d = x.shape
    return pl.pallas_call(
        _final_norm_kernel,
        grid=(n // TOKEN_TILE,),
        in_specs=[pl.BlockSpec((TOKEN_TILE, d), lambda i: (i, 0)),
                  pl.BlockSpec((1, d), lambda i: (0, 0))],
        out_specs=pl.BlockSpec((TOKEN_TILE, d), lambda i: (i, 0)),
        out_shape=jax.ShapeDtypeStruct((n, d), F32),
        compiler_params=_params(("parallel",)),
        name="final_norm",
    )(x, gain)


def _hgrn_rows(t_pad):
    n = t_pad // CHUNK
    best = max(c for c in range(1, n + 1) if n % c == 0 and c * CHUNK <= 2048)
    return best * CHUNK


def kernel(x_prompt, x_sample, cache_k, cache_v, state_hgrn, meta_tokens, mix_norm, w_in, lb_logits, hgrn_norm, sb_norm, w_out, ffn_norm, dense_w1, dense_w3, dense_w2, moe_router, moe_w1, moe_w3, moe_w2, final_norm):
    n_seq, seq, d = x_prompt.shape
    dec_b, dec_s, _ = x_sample.shape
    depth = w_in.shape[0]
    n_meta = meta_tokens.shape[0]
    heads = cache_k.shape[3]
    width = heads * HEAD
    t_real = n_meta + seq
    n_sample = dec_b * dec_s
    t_pad = next(t for t in range(_round_up(t_real, CHUNK), t_real + TOKEN_TILE * CHUNK, CHUNK)
                 if (n_seq * t + n_sample) % TOKEN_TILE == 0)
    n_prompt = n_seq * t_pad
    assert w_in.shape[2] == N_SEG * width and n_prompt % dec_s == 0 and dec_s % 8 == 0

    meta = jnp.broadcast_to(meta_tokens.astype(F32)[None], (n_seq, n_meta, d))
    slab = jnp.concatenate([meta, x_prompt, jnp.zeros((n_seq, t_pad - t_real, d), F32)], axis=1)
    x = jnp.concatenate([slab.reshape(n_prompt, d), x_sample.reshape(n_sample, d)], axis=0)

    p_lb = jax.nn.softmax(lb_logits.astype(F32), axis=0)
    lb_all = (jnp.cumsum(p_lb, axis=0) - p_lb[0]).reshape(depth, 1, width)

    w_in_bf = w_in.astype(BF16)
    w_out_bf = w_out.astype(BF16).reshape(depth, 2, width, d)
    dense_bf = [w.astype(BF16) for w in (dense_w1, dense_w3, dense_w2)]
    moe_bf = [w.astype(BF16) for w in (moe_w1, moe_w3, moe_w2)]
    n_experts = moe_router.shape[-1]
    router = jnp.pad(moe_router.astype(F32), ((0, 0), (0, 0), (0, HEAD - n_experts)))
    mix_g = mix_norm.reshape(depth, 1, d)
    ffn_g = ffn_norm.reshape(depth, 1, d)
    hgrn_g = hgrn_norm.reshape(depth, 1, HEAD)
    sb_g = sb_norm.reshape(depth, 1, HEAD)
    hgrn_m = jnp.asarray(_hgrn_masks(), BF16)
    sb_u = jnp.asarray(_sb_masks(), BF16)
    rows_p = _hgrn_rows(t_pad)

    kp, vp, sp, ks, vs, ss = [], [], [], [], [], []
    for l in range(depth):
        p7 = _inproj(x, mix_g, w_in_bf, l, width)
        ya_p, s_p = _hgrn(p7, lb_all, hgrn_g, hgrn_m, l, n_seq=n_seq, rows=rows_p, n_blk=t_pad // rows_p,
                          row_blk0=0, t_valid=t_real)
        ya_s, s_s = _hgrn(p7, lb_all, hgrn_g, hgrn_m, l, n_seq=dec_b, rows=dec_s, n_blk=1,
                          row_blk0=n_prompt // dec_s, t_valid=dec_s, s0=state_hgrn)
        yb_p = _sb_prompt(p7, sb_g, sb_u, l, n_seq=n_seq, t_pad=t_pad)
        yb_s = _sb_sample(p7, cache_k, cache_v, sb_g, sb_u, l, n_seq=dec_b, rows=dec_s,
                          row_blk0=n_prompt // dec_s)
        x = _outproj((ya_p, yb_p), (ya_s, yb_s), w_out_bf, x, l)
        if l % 2 == 0:
            x = _ffn(x, ffn_g, *dense_bf, l, l // 2)
        else:
            x = _moe(x, ffn_g, router, *moe_bf, l, l // 2, n_experts)

        kv = lambda seg: p7[seg, :n_prompt].reshape(n_seq, t_pad, heads, HEAD)[:, :t_real]
        kv_s = lambda seg: p7[seg, n_prompt:n_prompt + n_sample].reshape(dec_b, dec_s, heads, HEAD)
        kp.append(kv(5))
        vp.append(kv(6))
        sp.append(s_p)
        ks.append(kv_s(5))
        vs.append(kv_s(6))
        ss.append(s_s)

    y = _final_norm(x, final_norm.reshape(1, d))
    y_prompt = y[:n_prompt].reshape(n_seq, t_pad, d)[:, n_meta:t_real]
    y_sample = y[n_prompt:n_prompt + n_sample].reshape(dec_b, dec_s, d)
    return (y_prompt, y_sample, jnp.stack(kp), jnp.stack(vp), jnp.stack(sp),
            jnp.stack(ks), jnp.stack(vs), jnp.stack(ss))
```

```python
import functools

import numpy as np
import jax
import jax.numpy as jnp
from jax import lax
from jax.experimental import pallas as pl
from jax.experimental.pallas import tpu as pltpu

F32 = jnp.float32
BF16 = jnp.bfloat16
EPS = 1e-6
LB_FLOOR = 1e-20
HEAD = 128
CHUNK = 128
LEVELS = 7
N_SEG = 7
TOKEN_TILE = 512
COL_TILE = 2048
IN_ROW_TILE = 1024
IN_COL_TILE = 1024
MOE_ROWS = 512
FF_TILE = 512
VMEM_LIMIT = 56 * 1024 * 1024
HGRN_GROUP = 4
SB_GROUP = 11
SAMPLE_KEYS = 512
SB_EXIT = -104.0
ROUTE_I1, ROUTE_I2, ROUTE_G1, ROUTE_G2 = 8, 9, 10, 11


def _round_up(x, m):
    return (x + m - 1) // m * m


def _params(sem):
    return pltpu.CompilerParams(dimension_semantics=sem, vmem_limit_bytes=VMEM_LIMIT)


def _dot(a, b):
    return jnp.dot(a, b, preferred_element_type=F32)


def _dot_nt(a, b):
    return lax.dot_general(a, b, (((1,), (1,)), ((), ())), preferred_element_type=F32)


def _split2(x):
    hi = x.astype(BF16)
    lo = (x - hi.astype(F32)).astype(BF16)
    return hi, lo


def _rms_rows(x, g):
    return x * lax.rsqrt(jnp.mean(x * x, axis=-1, keepdims=True) + EPS) * g


def _inproj_kernel(x_ref, g_ref, w_ref, o_ref, h_ref):
    @pl.when(pl.program_id(1) == 0)
    def _():
        h_ref[...] = _rms_rows(x_ref[...], g_ref[...]).astype(BF16)

    o_ref[...] = _dot(h_ref[...], w_ref[...])


def _inproj(x, gain, w_bf, layer, width):
    n, d = x.shape
    per_seg = width // IN_COL_TILE
    rows = IN_ROW_TILE if n % IN_ROW_TILE == 0 else TOKEN_TILE
    grid = (n // rows, N_SEG * per_seg)
    return pl.pallas_call(
        _inproj_kernel,
        grid=grid,
        in_specs=[
            pl.BlockSpec((rows, d), lambda i, j: (i, 0)),
            pl.BlockSpec((None, 1, d), lambda i, j: (layer, 0, 0)),
            pl.BlockSpec((None, d, IN_COL_TILE), lambda i, j: (layer, 0, j)),
        ],
        out_specs=pl.BlockSpec((None, rows, IN_COL_TILE), lambda i, j: (j // per_seg, i, j % per_seg)),
        out_shape=jax.ShapeDtypeStruct((N_SEG, n, width), F32),
        scratch_shapes=[pltpu.VMEM((rows, d), BF16)],
        compiler_params=_params(("parallel", "arbitrary")),
        name="inproj",
    )(x, gain, w_bf)


def _hgrn_masks():
    c = CHUNK
    t = np.arange(c)[:, None]
    j = np.arange(c)[None, :]
    blocks = [(j <= t)]
    for l in range(LEVELS):
        hs = 1 << l
        mid = (t >> (l + 1) << (l + 1)) + hs
        upper = ((t >> l) & 1) == 1
        blocks.append(np.where(upper, (j >= mid) & (j <= t), (j > t) & (j <= mid - 1)))
    blocks.append(j > t)
    return np.concatenate(blocks, axis=0).astype(np.float32)


def _hgrn_chunk(q, fa, v, ga, valid, lbc, gain, m_ref, st_ref, group):
    log_lb, log1m_lb, one_m_lb = lbc
    e = jnp.exp(-jnp.abs(fa))
    log_sig = jnp.minimum(fa, 0.0) - jnp.log1p(e)
    bterm = log1m_lb + log_sig
    lf = jnp.maximum(log_lb, bterm) + jnp.log1p(jnp.exp(-jnp.abs(log_lb - bterm)))
    k = one_m_lb * (jnp.where(fa > 0.0, e, 1.0) / (1.0 + e))
    lf = jnp.where(valid, lf, 0.0)
    k = jnp.where(valid, k, 0.0)

    lf_hi, lf_lo = _split2(lf)
    m = m_ref[...]
    dall = _dot(m, lf_hi) + _dot(m, lf_lo)
    b = dall[0:CHUNK]
    d_last = dall[(LEVELS + 1) * CHUNK:(LEVELS + 2) * CHUNK]

    heads = [slice(hh * HEAD, (hh + 1) * HEAD) for hh in range(group)]
    row = lax.broadcasted_iota(jnp.int32, q.shape, 0)
    ti = lax.broadcasted_iota(jnp.int32, (CHUNK, CHUNK), 0)
    si = lax.broadcasted_iota(jnp.int32, (CHUNK, CHUNK), 1)
    top_bit = 31 - lax.clz(ti ^ si)
    owner = jnp.where(si < ti, top_bit, jnp.where(si == ti, -1, -2))

    q_bf = q.astype(BF16)
    k_bf = k.astype(BF16)
    a = [jnp.where(owner == -1, _dot_nt(q_bf[:, hd], k_bf[:, hd]), 0.0) for hd in heads]
    for l in range(LEVELS):
        upper = ((row >> l) & 1) == 1
        z = (jnp.where(upper, q, k) * jnp.exp(dall[(l + 1) * CHUNK:(l + 2) * CHUNK])).astype(BF16)
        a = [jnp.where(owner == l, _dot_nt(z[:, hd], z[:, hd]), a[hh]) for hh, hd in enumerate(heads)]

    v_bf = v.astype(BF16)
    qe = (q * jnp.exp(b)).astype(BF16)
    w = (k * jnp.exp(d_last)).astype(BF16)
    keep = jnp.exp(b[CHUNK - 1:CHUNK, :])
    st = [st_ref[hh] for hh in range(group)]
    o = [_dot(a[hh].astype(BF16), v_bf[:, hd]) + _dot_nt(qe[:, hd], st[hh].astype(BF16))
         for hh, hd in enumerate(heads)]
    for hh, hd in enumerate(heads):
        st_ref[hh] = st[hh] * keep[:, hd] + _dot(v[:, hd].T.astype(BF16), w[:, hd])

    y = jnp.concatenate([_rms_rows(oh, gain) for oh in o], axis=1)
    return (y * (ga / (1.0 + jnp.exp(-ga)))).astype(BF16)


def _hgrn_kernel(*refs, rows, t_valid, group, has_s0):
    q_ref, f_ref, i_ref, g_ref, lb_ref, gain_ref, m_ref = refs[:7]
    s0_ref = refs[7] if has_s0 else None
    y_ref, s_ref, st_ref = refs[-3:]
    r = pl.program_id(2)

    @pl.when(r == 0)
    def _():
        for hh in range(group):
            st_ref[hh] = s0_ref[hh].T if has_s0 else jnp.zeros((HEAD, HEAD), F32)

    gain = gain_ref[...]
    lb = lb_ref[...]
    lbc = (jnp.log(jnp.maximum(lb, LB_FLOOR)), jnp.log1p(-lb), 1.0 - lb)
    row = lax.broadcasted_iota(jnp.int32, (CHUNK, group * HEAD), 0)

    if rows < CHUNK:
        pad = jnp.zeros((CHUNK - rows, group * HEAD), F32)
        ld = lambda ref: jnp.concatenate([ref[...], pad], axis=0)
        y = _hgrn_chunk(ld(q_ref), ld(f_ref), ld(i_ref), ld(g_ref), row < t_valid, lbc, gain, m_ref, st_ref, group)
        y_ref[...] = y[:rows]
    else:
        def body(c, carry):
            sl = pl.ds(pl.multiple_of(c * CHUNK, CHUNK), CHUNK)
            valid = (r * rows + c * CHUNK + row) < t_valid
            y_ref[sl, :] = _hgrn_chunk(q_ref[sl, :], f_ref[sl, :], i_ref[sl, :], g_ref[sl, :],
                                       valid, lbc, gain, m_ref, st_ref, group)
            return carry

        lax.fori_loop(0, rows // CHUNK, body, 0)

    @pl.when(r == pl.num_programs(2) - 1)
    def _():
        for hh in range(group):
            s_ref[hh] = st_ref[hh].T


def _hgrn(p7, lb, gain, masks, layer, *, n_seq, rows, n_blk, row_blk0, t_valid, s0=None):
    _, n, width = p7.shape
    heads = width // HEAD
    group = HGRN_GROUP
    gw = group * HEAD
    seg_spec = lambda seg: pl.BlockSpec((None, rows, gw),
                                        lambda b, h, r: (seg, row_blk0 + b * n_blk + r, h))
    in_specs = [seg_spec(0), seg_spec(1), seg_spec(2), seg_spec(3),
                pl.BlockSpec((None, 1, gw), lambda b, h, r: (layer, 0, h)),
                pl.BlockSpec((None, 1, HEAD), lambda b, h, r: (layer, 0, 0)),
                pl.BlockSpec(masks.shape, lambda b, h, r: (0, 0))]
    args = [p7, p7, p7, p7, lb, gain, masks]
    if s0 is not None:
        in_specs.append(pl.BlockSpec((None, None, group, HEAD, HEAD), lambda b, h, r: (layer, b, h, 0, 0)))
        args.append(s0)
    kern = functools.partial(_hgrn_kernel, rows=rows, t_valid=t_valid, group=group, has_s0=s0 is not None)
    return pl.pallas_call(
        kern,
        grid=(n_seq, heads // group, n_blk),
        in_specs=in_specs,
        out_specs=[pl.BlockSpec((rows, gw), lambda b, h, r: (b * n_blk + r, h)),
                   pl.BlockSpec((None, group, HEAD, HEAD), lambda b, h, r: (b, h, 0, 0))],
        out_shape=[jax.ShapeDtypeStruct((n_seq * n_blk * rows, width), BF16),
                   jax.ShapeDtypeStruct((n_seq, heads, HEAD, HEAD), F32)],
        scratch_shapes=[pltpu.VMEM((group, HEAD, HEAD), F32)],
        compiler_params=_params(("parallel", "parallel", "arbitrary")),
        name="hgrn_sample" if s0 is not None else "hgrn_prompt",
    )(*args)


def _sb_masks():
    j = np.arange(CHUNK)[:, None]
    s = np.arange(CHUNK)[None, :]
    return np.concatenate([np.ones((CHUNK, CHUNK)), (j > s)], axis=1).astype(np.float32)


def _sb_tiles(q_bf, k_t, v_t, mask, u_ref, r_ref, acc_ref, scale):
    g, tq, _ = q_bf.shape
    z = lax.dot_general(q_bf, k_t.astype(BF16), (((2,), (2,)), ((0,), (0,))), preferred_element_type=F32) * scale
    log_beta = jnp.minimum(z, 0.0) - jnp.log1p(jnp.exp(-jnp.abs(z)))
    log_rest = log_beta - z
    if mask is not None:
        log_rest = jnp.where(mask, log_rest, 0.0)
    hi, lo = _split2(log_rest.reshape(g * tq, CHUNK))
    u = u_ref[...]
    c = (_dot(hi, u) + _dot(lo, u)).reshape(g, tq, 2 * CHUNK)
    r = r_ref[...]
    w = jnp.exp(log_beta + r + c[:, :, CHUNK:])
    if mask is not None:
        w = jnp.where(mask, w, 0.0)
    acc_ref[...] += lax.dot_general(w.astype(BF16), v_t.astype(BF16), (((2,), (1,)), ((0,), (0,))),
                                    preferred_element_type=F32)
    r_ref[...] = r + c[:, :, :CHUNK]


def _sb_prompt_kernel(q_ref, k_ref, v_ref, gain_ref, u_ref, y_ref, r_ref, acc_ref, *, scale, sub):
    first = pl.program_id(2) * sub
    r_ref[...] = jnp.zeros_like(r_ref)
    acc_ref[...] = jnp.zeros_like(acc_ref)
    ti = lax.broadcasted_iota(jnp.int32, (sub, CHUNK, CHUNK), 1)
    si = lax.broadcasted_iota(jnp.int32, (sub, CHUNK, CHUNK), 2)
    ai = lax.broadcasted_iota(jnp.int32, (sub, CHUNK, CHUNK), 0)
    q_bf = q_ref[...].astype(BF16).reshape(sub, CHUNK, HEAD)

    def tiles(step, mask):
        rows = [pl.ds(pl.multiple_of(jnp.maximum(first + a - step, 0) * CHUNK, CHUNK), CHUNK) for a in range(sub)]
        k_t = jnp.stack([k_ref[sl, :] for sl in rows])
        v_t = jnp.stack([v_ref[sl, :] for sl in rows])
        _sb_tiles(q_bf, k_t, v_t, mask, u_ref, r_ref, acc_ref, scale)

    tiles(0, si < ti)

    def cond(c):
        step, worst = c
        return jnp.logical_and(step < first + sub, worst > SB_EXIT)

    def body(c):
        step, _ = c
        tiles(step, first + ai - step >= 0)
        return step + 1, jnp.max(r_ref[...])

    lax.while_loop(cond, body, (jnp.int32(1), jnp.max(r_ref[...])))
    y = _rms_rows(acc_ref[...], gain_ref[...]).astype(BF16)
    y_ref[...] = y.reshape(sub * CHUNK, HEAD)


def _sb_prompt(p7, gain, u, layer, *, n_seq, t_pad):
    _, n, width = p7.shape
    heads = width // HEAD
    n_q = t_pad // CHUNK
    sub = max(c for c in range(1, SB_GROUP + 1) if n_q % c == 0)
    n_g = n_q // sub
    kern = functools.partial(_sb_prompt_kernel, scale=HEAD ** -0.5, sub=sub)
    return pl.pallas_call(
        kern,
        grid=(n_seq, heads, n_g),
        in_specs=[pl.BlockSpec((None, sub * CHUNK, HEAD), lambda b, h, i: (4, b * n_g + i, h)),
                  pl.BlockSpec((None, t_pad, HEAD), lambda b, h, i: (5, b, h)),
                  pl.BlockSpec((None, t_pad, HEAD), lambda b, h, i: (6, b, h)),
                  pl.BlockSpec((None, 1, HEAD), lambda b, h, i: (layer, 0, 0)),
                  pl.BlockSpec(u.shape, lambda b, h, i: (0, 0))],
        out_specs=pl.BlockSpec((sub * CHUNK, HEAD), lambda b, h, i: (b * n_g + i, h)),
        out_shape=jax.ShapeDtypeStruct((n_seq * t_pad, width), BF16),
        scratch_shapes=[pltpu.VMEM((sub, CHUNK, CHUNK), F32), pltpu.VMEM((sub, CHUNK, HEAD), F32)],
        compiler_params=_params(("parallel", "parallel", "arbitrary")),
        name="sb_prompt",
    )(p7, p7, p7, gain, u)


def _sb_sample_kernel(q_ref, kn_ref, vn_ref, kc_ref, vc_ref, gain_ref, u_ref, y_ref,
                      qb_ref, r_ref, acc_ref, *, scale, rows, heads, n_cache):
    kt = pl.program_id(1)
    n_kt = pl.num_programs(1)
    cols = [slice(h * HEAD, (h + 1) * HEAD) for h in range(heads)]
    ti = lax.broadcasted_iota(jnp.int32, (heads, rows, CHUNK), 1)
    si = lax.broadcasted_iota(jnp.int32, (heads, rows, CHUNK), 2)

    @pl.when(kt == 0)
    def _():
        qb_ref[...] = jnp.stack([q_ref[:, c] for c in cols]).astype(BF16)
        r_ref[...] = jnp.zeros_like(r_ref)
        acc_ref[...] = jnp.zeros_like(acc_ref)
        pad = jnp.zeros((CHUNK - rows, HEAD), F32)
        k_t = jnp.stack([jnp.concatenate([kn_ref[:, c], pad], axis=0) for c in cols])
        v_t = jnp.stack([jnp.concatenate([vn_ref[:, c], pad], axis=0) for c in cols])
        _sb_tiles(qb_ref[...], k_t, v_t, si < ti, u_ref, r_ref, acc_ref, scale)

    block_left = n_cache - (n_kt - 1 - kt) * SAMPLE_KEYS
    for part in reversed(range(SAMPLE_KEYS // CHUNK)):
        left = block_left - part * CHUNK
        rows_k = slice(part * CHUNK, (part + 1) * CHUNK)

        @pl.when(jnp.logical_and(left > 0, jnp.max(r_ref[...]) > SB_EXIT))
        def _():
            live = lax.broadcasted_iota(jnp.int32, (heads, CHUNK, HEAD), 1) < left
            k_t = jnp.where(live, jnp.stack([kc_ref[rows_k, h, :] for h in range(heads)]), 0.0)
            v_t = jnp.where(live, jnp.stack([vc_ref[rows_k, h, :] for h in range(heads)]), 0.0)
            _sb_tiles(qb_ref[...], k_t, v_t, si < left, u_ref, r_ref, acc_ref, scale)

    @pl.when(kt == n_kt - 1)
    def _():
        gain = gain_ref[...]
        y_ref[...] = jnp.concatenate([_rms_rows(acc_ref[h], gain) for h in range(heads)], axis=1).astype(BF16)


def _sb_sample(p7, cache_k, cache_v, gain, u, layer, *, n_seq, rows, row_blk0):
    _, n, width = p7.shape
    n_cache, heads = cache_k.shape[2:4]
    assert rows <= CHUNK and width == heads * HEAD
    n_kt = pl.cdiv(n_cache, SAMPLE_KEYS)
    kern = functools.partial(_sb_sample_kernel, scale=HEAD ** -0.5, rows=rows, heads=heads, n_cache=n_cache)
    new_spec = lambda seg: pl.BlockSpec((None, rows, width), lambda b, t: (seg, row_blk0 + b, 0))
    cache_spec = pl.BlockSpec((None, None, SAMPLE_KEYS, heads, HEAD), lambda b, t: (layer, b, n_kt - 1 - t, 0, 0))
    return pl.pallas_call(
        kern,
        grid=(n_seq, n_kt),
        in_specs=[new_spec(4), new_spec(5), new_spec(6), cache_spec, cache_spec,
                  pl.BlockSpec((None, 1, HEAD), lambda b, t: (layer, 0, 0)),
                  pl.BlockSpec(u.shape, lambda b, t: (0, 0))],
        out_specs=pl.BlockSpec((rows, width), lambda b, t: (b, 0)),
        out_shape=jax.ShapeDtypeStruct((n_seq * rows, width), BF16),
        scratch_shapes=[pltpu.VMEM((heads, rows, HEAD), BF16), pltpu.VMEM((heads, rows, CHUNK), F32),
                        pltpu.VMEM((heads, rows, HEAD), F32)],
        compiler_params=_params(("parallel", "arbitrary")),
        name="sb_sample",
    )(p7, p7, p7, cache_k, cache_v, gain, u)


def _outproj_kernel(yap_ref, ybp_ref, yas_ref, ybs_ref, wa_ref, wb_ref, x_ref, o_ref, *, prompt_tiles):
    def mix(ya_ref, yb_ref):
        o_ref[...] = x_ref[...] + _dot(ya_ref[...], wa_ref[...]) + _dot(yb_ref[...], wb_ref[...])

    is_prompt = pl.program_id(0) < prompt_tiles
    pl.when(is_prompt)(lambda: mix(yap_ref, ybp_ref))
    pl.when(jnp.logical_not(is_prompt))(lambda: mix(yas_ref, ybs_ref))


def _outproj(y_prompt, y_sample, w_bf, x, layer):
    n, d = x.shape
    width = y_prompt[0].shape[1]
    pt, st = y_prompt[0].shape[0] // TOKEN_TILE, y_sample[0].shape[0] // TOKEN_TILE
    assert pt * TOKEN_TILE == y_prompt[0].shape[0] and st * TOKEN_TILE == y_sample[0].shape[0] and pt + st == n // TOKEN_TILE
    p_spec = pl.BlockSpec((TOKEN_TILE, width), lambda i, j: (jnp.minimum(i, pt - 1), 0))
    s_spec = pl.BlockSpec((TOKEN_TILE, width), lambda i, j: (jnp.maximum(i - pt, 0), 0))
    return pl.pallas_call(
        functools.partial(_outproj_kernel, prompt_tiles=pt),
        grid=(n // TOKEN_TILE, d // COL_TILE),
        in_specs=[p_spec, p_spec, s_spec, s_spec,
                  pl.BlockSpec((None, None, width, COL_TILE), lambda i, j: (layer, 0, 0, j)),
                  pl.BlockSpec((None, None, width, COL_TILE), lambda i, j: (layer, 1, 0, j)),
                  pl.BlockSpec((TOKEN_TILE, COL_TILE), lambda i, j: (i, j))],
        out_specs=pl.BlockSpec((TOKEN_TILE, COL_TILE), lambda i, j: (i, j)),
        out_shape=jax.ShapeDtypeStruct((n, d), F32),
        input_output_aliases={6: 0},
        compiler_params=_params(("parallel", "arbitrary")),
        name="outproj",
    )(*y_prompt, *y_sample, w_bf, w_bf, x)


def _swiglu_step(h, w1_ref, w3_ref, w2_ref):
    a = _dot(h, w1_ref[...])
    b = _dot(h, w3_ref[...])
    g = (a / (1.0 + jnp.exp(-a))) * b
    return _dot(g.astype(BF16), w2_ref[...])


def _ffn_kernel(x_ref, g_ref, w1_ref, w3_ref, w2_ref, o_ref, h_ref):
    @pl.when(pl.program_id(1) == 0)
    def _():
        x = x_ref[...]
        h_ref[...] = _rms_rows(x, g_ref[...]).astype(BF16)
        o_ref[...] = x

    o_ref[...] += _swiglu_step(h_ref[...], w1_ref, w3_ref, w2_ref)


def _ffn(x, gain, w1, w3, w2, layer, idx):
    n, d = x.shape
    ff = w1.shape[-1]
    return pl.pallas_call(
        _ffn_kernel,
        grid=(n // TOKEN_TILE, ff // FF_TILE),
        in_specs=[pl.BlockSpec((TOKEN_TILE, d), lambda i, f: (i, 0)),
                  pl.BlockSpec((None, 1, d), lambda i, f: (layer, 0, 0)),
                  pl.BlockSpec((None, d, FF_TILE), lambda i, f: (idx, 0, f)),
                  pl.BlockSpec((None, d, FF_TILE), lambda i, f: (idx, 0, f)),
                  pl.BlockSpec((None, FF_TILE, d), lambda i, f: (idx, f, 0))],
        out_specs=pl.BlockSpec((TOKEN_TILE, d), lambda i, f: (i, 0)),
        out_shape=jax.ShapeDtypeStruct((n, d), F32),
        scratch_shapes=[pltpu.VMEM((TOKEN_TILE, d), BF16)],
        compiler_params=_params(("parallel", "arbitrary")),
        name="ffn_dense",
    )(x, gain, w1, w3, w2)


def _route(h, rt_ref, n_experts):
    h_hi, h_lo = _split2(h)
    r = rt_ref[...]
    r_hi, r_lo = _split2(r)
    logits = _dot(h_hi, r_hi) + _dot(h_lo, r_hi) + _dot(h_hi, r_lo)
    lane = lax.broadcasted_iota(jnp.int32, logits.shape, 1)
    live = lane < n_experts
    logits = jnp.where(live, logits, -jnp.inf)
    ex = jnp.exp(logits - jnp.max(logits, axis=-1, keepdims=True))
    p = ex / jnp.sum(ex, axis=-1, keepdims=True)
    p = jnp.where(live, p, -1.0)
    p1 = jnp.max(p, axis=-1, keepdims=True)
    i1 = jnp.min(jnp.where(p == p1, lane, HEAD), axis=-1, keepdims=True)
    rest = jnp.where(lane == i1, -1.0, p)
    p2 = jnp.max(rest, axis=-1, keepdims=True)
    i2 = jnp.min(jnp.where(rest == p2, lane, HEAD), axis=-1, keepdims=True)
    tot = p1 + p2
    g1, g2 = p1 / tot, p2 / tot
    return jnp.where(lane == i1, g1, 0.0) + jnp.where(lane == i2, g2, 0.0), i1, i2, g1, g2


def _router_kernel(x_ref, g_ref, rt_ref, o_ref, *, n_experts):
    gates, i1, i2, g1, g2 = _route(_rms_rows(x_ref[...], g_ref[...]), rt_ref, n_experts)
    lane = lax.broadcasted_iota(jnp.int32, gates.shape, 1)
    rec = jnp.where(lane == ROUTE_I1, i1.astype(F32), gates)
    rec = jnp.where(lane == ROUTE_I2, i2.astype(F32), rec)
    rec = jnp.where(lane == ROUTE_G1, g1, rec)
    o_ref[...] = jnp.where(lane == ROUTE_G2, g2, rec)


def _router(x, gain, router, layer, idx, n_experts):
    n, d = x.shape
    assert n_experts <= ROUTE_I1
    return pl.pallas_call(
        functools.partial(_router_kernel, n_experts=n_experts),
        grid=(n // TOKEN_TILE,),
        in_specs=[pl.BlockSpec((TOKEN_TILE, d), lambda i: (i, 0)),
                  pl.BlockSpec((None, 1, d), lambda i: (layer, 0, 0)),
                  pl.BlockSpec((None, d, HEAD), lambda i: (idx, 0, 0))],
        out_specs=pl.BlockSpec((TOKEN_TILE, HEAD), lambda i: (i, 0)),
        out_shape=jax.ShapeDtypeStruct((n, HEAD), F32),
        compiler_params=_params(("parallel",)),
        name="moe_router",
    )(x, gain, router)


def _moe_tile(n_ff):
    return _round_up(MOE_ROWS, 8 * n_ff)


def _dispatch_plan(route, n_experts, tile):
    n = route.shape[0]
    n_pick = 2 * n
    n_tiles = -(-n_pick // tile) + n_experts
    e_flat = route[:, ROUTE_I1:ROUTE_I2 + 1].astype(jnp.int32).reshape(n_pick)
    onehot = (e_flat[:, None] == jnp.arange(n_experts, dtype=jnp.int32)[None, :]).astype(jnp.int32)
    csum = jnp.cumsum(onehot, axis=0)
    counts = csum[-1]
    rank = jnp.sum(onehot * csum, axis=1) - 1
    tiles_per = (counts + tile - 1) // tile
    tile_end = jnp.cumsum(tiles_per)
    tile_start = tile_end - tiles_per
    dest = jnp.sum(onehot * tile_start[None, :], axis=1) * tile + rank
    pick = jnp.zeros((n_tiles * tile,), jnp.int32).at[dest].set(jnp.arange(n_pick, dtype=jnp.int32))
    t = jnp.arange(n_tiles, dtype=jnp.int32)
    live = t < tile_end[-1]
    t_eff = jnp.minimum(t, tile_end[-1] - 1)
    tile_expert = jnp.sum((t_eff[:, None] >= tile_end[None, :]).astype(jnp.int32), axis=1)
    used = (t_eff - tile_start[tile_expert]) * tile
    tile_rows = jnp.where(live, jnp.clip(counts[tile_expert] - used, 0, tile), 0)
    return tile_expert, tile_rows.astype(jnp.int32), pick


def _moe_kernel(te_ref, rows_ref, pick_ref, x_hbm, g_ref, w1_ref, w3_ref, w2_ref, y_hbm,
                xg_ref, h_ref, acc_ref, yo_ref, sem_in, sem_out, *, n_tokens, tile, n_ff_static):
    i = pl.program_id(0)
    f = pl.program_id(1)
    n_tiles = pl.num_programs(0)
    n_ff = pl.num_programs(1)
    per_step = tile // n_ff_static
    rows = rows_ref[i]
    slot = i % 2
    nxt = jnp.minimum(i + 1, n_tiles - 1)
    next_live = jnp.logical_and(i + 1 < n_tiles, rows_ref[nxt] > 0)

    def row_in(src_tile, p, to):
        tok = pick_ref[src_tile * tile + p] >> 1
        return pltpu.make_async_copy(x_hbm.at[pl.ds(tok, 1)], xg_ref.at[to, pl.ds(p, 1)], sem_in.at[to])

    def rows_in_wait(to):
        pltpu.make_async_copy(x_hbm.at[pl.ds(0, tile)], xg_ref.at[to], sem_in.at[to]).wait()

    def row_out(p):
        pick = pick_ref[i * tile + p]
        dst = (pick & 1) * n_tokens + (pick >> 1)
        return pltpu.make_async_copy(yo_ref.at[pl.ds(p, 1)], y_hbm.at[pl.ds(dst, 1)], sem_out)

    def rows_out_wait(n):
        whole = pl.multiple_of(n // 8 * 8, 8)

        @pl.when(whole > 0)
        def _():
            pltpu.make_async_copy(yo_ref.at[pl.ds(0, whole)], y_hbm.at[pl.ds(0, whole)], sem_out).wait()

        each(whole, n, lambda p: row_out(p).wait())

    def each(lo, hi, fn):
        def body(p, c):
            fn(p)
            return c

        lax.fori_loop(lo, hi, body, 0)

    def each_by_8(lo, n, fn):
        def body(c, carry):
            for u in range(8):
                fn(lo + c * 8 + u)
            return carry

        lax.fori_loop(0, n // 8, body, 0)
        each(lo + n // 8 * 8, lo + n, fn)

    @pl.when(jnp.logical_and(i == 0, f == 0))
    def _():
        each_by_8(0, tile, lambda p: row_in(0, p, 0).start())

    @pl.when(rows > 0)
    def _():
        @pl.when(f == 0)
        def _():
            rows_in_wait(slot)
            h_ref[...] = _rms_rows(xg_ref[slot], g_ref[...]).astype(BF16)
            acc_ref[...] = jnp.zeros_like(acc_ref)

        y = _swiglu_step(h_ref[...], w1_ref, w3_ref, w2_ref)
        for u in range(per_step):
            row_in(nxt, f * per_step + u, 1 - slot).start()

        @pl.when(f < n_ff - 1)
        def _():
            acc_ref[...] += y

        @pl.when(f == n_ff - 1)
        def _():
            @pl.when(i > 0)
            def _():
                rows_out_wait(rows_ref[jnp.maximum(i - 1, 0)])

            yo_ref[...] = acc_ref[...] + y
            each_by_8(0, rows, lambda p: row_out(p).start())

            @pl.when(jnp.logical_not(next_live))
            def _():
                rows_out_wait(rows)
                rows_in_wait(1 - slot)


def _moe_experts(x, gain, plan, tile, w1, w3, w2, layer, idx):
    n, d = x.shape
    ff = w1.shape[-1]
    tile_expert, tile_rows, pick = plan
    n_tiles = tile_expert.shape[0]
    n_ff = ff // FF_TILE
    fblk = lambda i, f, rows: jnp.where(rows[i] > 0, f, n_ff - 1)
    grid_spec = pltpu.PrefetchScalarGridSpec(
        num_scalar_prefetch=3,
        grid=(n_tiles, n_ff),
        in_specs=[pl.BlockSpec(memory_space=pl.ANY),
                  pl.BlockSpec((None, 1, d), lambda i, f, te, rows, pk: (layer, 0, 0)),
                  pl.BlockSpec((None, None, d, FF_TILE), lambda i, f, te, rows, pk: (idx, te[i], 0, fblk(i, f, rows))),
                  pl.BlockSpec((None, None, d, FF_TILE), lambda i, f, te, rows, pk: (idx, te[i], 0, fblk(i, f, rows))),
                  pl.BlockSpec((None, None, FF_TILE, d), lambda i, f, te, rows, pk: (idx, te[i], fblk(i, f, rows), 0))],
        out_specs=pl.BlockSpec(memory_space=pl.ANY),
        scratch_shapes=[pltpu.VMEM((2, tile, d), F32), pltpu.VMEM((tile, d), BF16),
                        pltpu.VMEM((tile, d), F32), pltpu.VMEM((tile, d), F32),
                        pltpu.SemaphoreType.DMA((2,)), pltpu.SemaphoreType.DMA],
    )
    return pl.pallas_call(
        functools.partial(_moe_kernel, n_tokens=n, tile=tile, n_ff_static=n_ff),
        grid_spec=grid_spec,
        out_shape=jax.ShapeDtypeStruct((2 * n, d), F32),
        compiler_params=pltpu.CompilerParams(dimension_semantics=("arbitrary", "arbitrary"),
                                             vmem_limit_bytes=VMEM_LIMIT, disable_bounds_checks=True),
        name="moe_experts",
    )(tile_expert, tile_rows, pick, x, gain, w1, w3, w2)


def _moe_combine_kernel(x_ref, y1_ref, y2_ref, r_ref, o_ref):
    r = r_ref[...]
    o_ref[...] = (x_ref[...] + r[:, ROUTE_G1:ROUTE_G1 + 1] * y1_ref[...]
                  + r[:, ROUTE_G2:ROUTE_G2 + 1] * y2_ref[...])


def _moe_combine(x, y, route):
    n, d = x.shape
    nt = n // TOKEN_TILE
    return pl.pallas_call(
        _moe_combine_kernel,
        grid=(nt,),
        in_specs=[pl.BlockSpec((TOKEN_TILE, d), lambda i: (i, 0)),
                  pl.BlockSpec((TOKEN_TILE, d), lambda i: (i, 0)),
                  pl.BlockSpec((TOKEN_TILE, d), lambda i: (nt + i, 0)),
                  pl.BlockSpec((TOKEN_TILE, HEAD), lambda i: (i, 0))],
        out_specs=pl.BlockSpec((TOKEN_TILE, d), lambda i: (i, 0)),
        out_shape=jax.ShapeDtypeStruct((n, d), F32),
        input_output_aliases={0: 0},
        compiler_params=_params(("parallel",)),
        name="moe_combine",
    )(x, y, y, route)


def _moe(x, gain, router, w1, w3, w2, layer, idx, n_experts):
    route = _router(x, gain, router, layer, idx, n_experts)
    tile = _moe_tile(w1.shape[-1] // FF_TILE)
    y = _moe_experts(x, gain, _dispatch_plan(route, n_experts, tile), tile, w1, w3, w2, layer, idx)
    return _moe_combine(x, y, route)


def _final_norm_kernel(x_ref, g_ref, o_ref):
    o_ref[...] = _rms_rows(x_ref[...], g_ref[...])


def _final_norm(x, gain):
    n, d = x.shape
    return pl.pallas_call(
        _final_norm_kernel,
        grid=(n // TOKEN_TILE,),
        in_specs=[pl.BlockSpec((TOKEN_TILE, d), lambda i: (i, 0)),
                  pl.BlockSpec((1, d), lambda i: (0, 0))],
        out_specs=pl.BlockSpec((TOKEN_TILE, d), lambda i: (i, 0)),
        out_shape=jax.ShapeDtypeStruct((n, d), F32),
        compiler_params=_params(("parallel",)),
        name="final_norm",
    )(x, gain)


def _hgrn_rows(t_pad):
    n = t_pad // CHUNK
    best = max(c for c in range(1, n + 1) if n % c == 0 and c * CHUNK <= 2048)
    return best * CHUNK


def kernel(x_prompt, x_sample, cache_k, cache_v, state_hgrn, meta_tokens, mix_norm, w_in, lb_logits, hgrn_norm, sb_norm, w_out, ffn_norm, dense_w1, dense_w3, dense_w2, moe_router, moe_w1, moe_w3, moe_w2, final_norm):
    n_seq, seq, d = x_prompt.shape
    dec_b, dec_s, _ = x_sample.shape
    depth = w_in.shape[0]
    n_meta = meta_tokens.shape[0]
    heads = cache_k.shape[3]
    width = heads * HEAD
    t_real = n_meta + seq
    n_sample = dec_b * dec_s
    t_pad = next(t for t in range(_round_up(t_real, CHUNK), t_real + TOKEN_TILE * CHUNK, CHUNK)
                 if (n_seq * t + n_sample) % TOKEN_TILE == 0)
    n_prompt = n_seq * t_pad
    assert w_in.shape[2] == N_SEG * width and n_prompt % dec_s == 0 and dec_s % 8 == 0

    x = jnp.zeros((n_prompt + n_sample, d), F32)
    for b in range(n_seq):
        x = lax.dynamic_update_slice(x, meta_tokens.astype(F32), (b * t_pad, 0))
        x = lax.dynamic_update_slice(x, x_prompt[b], (b * t_pad + n_meta, 0))
    x = lax.dynamic_update_slice(x, x_sample.reshape(n_sample, d), (n_prompt, 0))

    p_lb = jax.nn.softmax(lb_logits.astype(F32), axis=0)
    lb_all = (jnp.cumsum(p_lb, axis=0) - p_lb[0]).reshape(depth, 1, width)

    w_in_bf = w_in.astype(BF16)
    w_out_bf = w_out.astype(BF16).reshape(depth, 2, width, d)
    dense_bf = [w.astype(BF16) for w in (dense_w1, dense_w3, dense_w2)]
    moe_bf = [w.astype(BF16) for w in (moe_w1, moe_w3, moe_w2)]
    n_experts = moe_router.shape[-1]
    router = jnp.pad(moe_router.astype(F32), ((0, 0), (0, 0), (0, HEAD - n_experts)))
    mix_g = mix_norm.reshape(depth, 1, d)
    ffn_g = ffn_norm.reshape(depth, 1, d)
    hgrn_g = hgrn_norm.reshape(depth, 1, HEAD)
    sb_g = sb_norm.reshape(depth, 1, HEAD)
    hgrn_m = jnp.asarray(_hgrn_masks(), BF16)
    sb_u = jnp.asarray(_sb_masks(), BF16)
    rows_p = _hgrn_rows(t_pad)

    kp, vp, sp, ks, vs, ss = [], [], [], [], [], []
    for l in range(depth):
        p7 = _inproj(x, mix_g, w_in_bf, l, width)
        ya_p, s_p = _hgrn(p7, lb_all, hgrn_g, hgrn_m, l, n_seq=n_seq, rows=rows_p, n_blk=t_pad // rows_p,
                          row_blk0=0, t_valid=t_real)
        ya_s, s_s = _hgrn(p7, lb_all, hgrn_g, hgrn_m, l, n_seq=dec_b, rows=dec_s, n_blk=1,
                          row_blk0=n_prompt // dec_s, t_valid=dec_s, s0=state_hgrn)
        yb_p = _sb_prompt(p7, sb_g, sb_u, l, n_seq=n_seq, t_pad=t_pad)
        yb_s = _sb_sample(p7, cache_k, cache_v, sb_g, sb_u, l, n_seq=dec_b, rows=dec_s,
                          row_blk0=n_prompt // dec_s)
        x = _outproj((ya_p, yb_p), (ya_s, yb_s), w_out_bf, x, l)
        if l % 2 == 0:
            x = _ffn(x, ffn_g, *dense_bf, l, l // 2)
        else:
            x = _moe(x, ffn_g, router, *moe_bf, l, l // 2, n_experts)

        kv = lambda seg: p7[seg, :n_prompt].reshape(n_seq, t_pad, heads, HEAD)[:, :t_real]
        kv_s = lambda seg: p7[seg, n_prompt:n_prompt + n_sample].reshape(dec_b, dec_s, heads, HEAD)
        kp.append(kv(5))
        vp.append(kv(6))
        sp.append(s_p)
        ks.append(kv_s(5))
        vs.append(kv_s(6))
        ss.append(s_s)

    y = _final_norm(x, final_norm.reshape(1, d))
    y_prompt = y[:n_prompt].reshape(n_seq, t_pad, d)[:, n_meta:t_real]
    y_sample = y[n_prompt:n_prompt + n_sample].reshape(dec_b, dec_s, d)
    return (y_prompt, y_sample, jnp.stack(kp), jnp.stack(vp), jnp.stack(sp),
            jnp.stack(ks), jnp.stack(vs), jnp.stack(ss))
```
